```python
import jax
import jax.numpy as jnp
from jax import lax
import numpy as np

D_MODEL = 1024
BATCH = 8
SEQ = 4096
DEPTH = 4

GRID_W = 64
CTX_LEN = 256
N_MIXERS = 3
EPS = 1e-6
D_FF = 4 * D_MODEL
N_DIRS = 2

GLA_HEADS = 4
GLA_DK = D_MODEL // 2
GLA_DV = D_MODEL
GLA_HK = GLA_DK // GLA_HEADS
GLA_HV = GLA_DV // GLA_HEADS
GLA_IN = 2 * GLA_DK + 2 * GLA_DV
GLA_GATE_RANK = 16
GLA_TAU = 16.0
GLA_CHUNK = 64

MLSTM_HEADS = 4
MLSTM_INNER = 2 * D_MODEL
MLSTM_HD = MLSTM_INNER // MLSTM_HEADS
MLSTM_QKV_BLOCK = 4
MLSTM_N_BLOCKS = MLSTM_INNER // MLSTM_QKV_BLOCK
MLSTM_CONV = 4
MLSTM_CHUNK = 64

POOL_WINDOWS = (2, 4, 8, 16)
POOL_GROUP = D_MODEL // len(POOL_WINDOWS)

kernel_name = 'hybrid_gla_mlstm_pool_flow_block'


def rmsnorm(x, g):
    xf = x.astype(jnp.float32)
    y = xf * lax.rsqrt(jnp.mean(xf * xf, axis=-1, keepdims=True) + EPS)
    return (y * g.astype(jnp.float32)).astype(x.dtype)


def modulate(h, shift, scale):
    return h * (1 + scale) + shift


def head_rmsnorm(o, g):
    return o * lax.rsqrt(jnp.mean(o * o, axis=-1, keepdims=True) + EPS) * g.astype(jnp.float32)


def head_layernorm(o, g):
    mu = jnp.mean(o, axis=-1, keepdims=True)
    oc = o - mu
    return oc * lax.rsqrt(jnp.mean(oc * oc, axis=-1, keepdims=True) + EPS) * g.astype(jnp.float32)


def to_chunks(t, size):
    b, n = t.shape[0], t.shape[1] // size
    t = t.reshape((b, n, size) + t.shape[2:])
    return t.transpose((1, 0, 3, 2) + tuple(range(4, t.ndim)))


def from_chunks(t):
    n, b, h, c, d = t.shape
    return t.transpose(1, 0, 3, 2, 4).reshape(b, n * c, h, d)


def gla_chunk_scan(q, k, v, log_a, s0):
    f32 = jnp.float32
    qc, kc, vc, gc = (to_chunks(t.astype(f32), GLA_CHUNK) for t in (q, k, v, log_a))
    tri = jnp.tril(jnp.ones((GLA_CHUNK, GLA_CHUNK), dtype=bool))

    def step(s, blk):
        qb, kb, vb, gb = blk
        b = jnp.cumsum(gb, axis=2)
        b_last = b[:, :, -1:, :]
        q_dec = qb * jnp.exp(b)
        att = jnp.where(tri, jnp.einsum('bhtd,bhsd->bhts', q_dec, kb * jnp.exp(-b)), 0.0)
        o = jnp.einsum('bhts,bhsv->bhtv', att, vb) + jnp.einsum('bhtd,bhdv->bhtv', q_dec, s)
        s_new = (jnp.exp(b_last[:, :, 0, :])[..., None] * s
                 + jnp.einsum('bhsd,bhsv->bhdv', kb * jnp.exp(b_last - b), vb))
        return s_new, o

    s_fin, o = lax.scan(step, s0, (qc, kc, vc, gc))
    return from_chunks(o), s_fin


def mlstm_chunk_scan(q, k, v, log_i, log_f, state):
    f32 = jnp.float32
    qc, kc, vc = (to_chunks(t.astype(f32), MLSTM_CHUNK) for t in (q, k, v))
    ic, fc = (to_chunks(t.astype(f32), MLSTM_CHUNK) for t in (log_i, log_f))
    tri = jnp.tril(jnp.ones((MLSTM_CHUNK, MLSTM_CHUNK), dtype=bool))

    def step(carry, blk):
        c_bar, n_bar, m_prev = carry
        qb, kb, vb, ib, fb = blk
        b = jnp.cumsum(fb, axis=-1)
        d_mat = jnp.where(tri, b[..., :, None] - b[..., None, :] + ib[..., None, :], -jnp.inf)
        inter = b + m_prev[..., None]
        m_row = jnp.maximum(jnp.max(d_mat, axis=-1), inter)
        w_qk = jnp.einsum('bhtd,bhsd->bhts', qb, kb) * jnp.exp(d_mat - m_row[..., None])
        w_inter = jnp.exp(inter - m_row)
        num = (jnp.einsum('bhts,bhsv->bhtv', w_qk, vb)
               + w_inter[..., None] * jnp.einsum('bhtd,bhdv->bhtv', qb, c_bar))
        den = jnp.sum(w_qk, axis=-1) + w_inter * jnp.einsum('bhtd,bhd->bht', qb, n_bar)
        h = num / jnp.maximum(jnp.abs(den), jnp.exp(-m_row))[..., None]
        carry_log = b[..., -1] + m_prev
        tok_log = b[..., -1:] - b + ib
        m_new = jnp.maximum(carry_log, jnp.max(tok_log, axis=-1))
        w_tok = jnp.exp(tok_log - m_new[..., None])
        w_carry = jnp.exp(carry_log - m_new)
        k_w = kb * w_tok[..., None]
        c_new = w_carry[..., None, None] * c_bar + jnp.einsum('bhsd,bhsv->bhdv', k_w, vb)
        n_new = w_carry[..., None] * n_bar + jnp.sum(k_w, axis=2)
        return (c_new, n_new, m_new), h

    state_fin, h = lax.scan(step, state, (qc, kc, vc, ic, fc))
    return from_chunks(h), state_fin


def bidirectional_scan(scan_fn, ctx_dirs, lat_dirs, init):
    rev = lambda t: jnp.flip(t, axis=1)
    out_ctx, out_lat = [], []
    for direction in range(N_DIRS):
        ca, la = ctx_dirs[direction], lat_dirs[direction]
        if direction == 1:
            ca = tuple(rev(t) for t in ca)
            la = tuple(rev(t) for t in la)
        o_c, state = scan_fn(*ca, init)
        o_l, _ = scan_fn(*la, state)
        if direction == 1:
            o_c, o_l = rev(o_c), rev(o_l)
        out_ctx.append(o_c)
        out_lat.append(o_l)
    return out_ctx[0] + out_ctx[1], out_lat[0] + out_lat[1]


def gla_mixer(a_ctx, a_lat, w_in, w_a1, w_a2, b_a, g_head, w_o, need_ctx):
    def project(a):
        B, T, _ = a.shape
        q, k, v, r = jnp.split(a @ w_in, [GLA_DK, 2 * GLA_DK, 2 * GLA_DK + GLA_DV], axis=-1)
        q = q.reshape(B, T, GLA_HEADS, GLA_HK) * GLA_HK ** -0.5
        k = k.reshape(B, T, GLA_HEADS, GLA_HK)
        v = v.reshape(B, T, GLA_HEADS, GLA_HV)
        dirs = []
        for d in range(N_DIRS):
            z = (a @ w_a1[d]) @ w_a2[d] + b_a[d]
            log_a = jax.nn.log_sigmoid(z.astype(jnp.float32)) / GLA_TAU
            dirs.append((q, k, v, log_a.reshape(B, T, GLA_HEADS, GLA_HK)))
        return dirs, r

    def finish(o, r):
        B, T = o.shape[:2]
        o = head_rmsnorm(o, g_head).reshape(B, T, GLA_DV).astype(r.dtype)
        return (o * jax.nn.silu(r)) @ w_o

    ctx_dirs, r_ctx = project(a_ctx)
    lat_dirs, r_lat = project(a_lat)
    s0 = jnp.zeros((a_lat.shape[0], GLA_HEADS, GLA_HK, GLA_HV), jnp.float32)
    o_ctx, o_lat = bidirectional_scan(gla_chunk_scan, ctx_dirs, lat_dirs, s0)
    out_ctx = finish(o_ctx, r_ctx) if need_ctx else None
    return out_ctx, finish(o_lat, r_lat)


def centred_conv(x, w, b):
    K = w.shape[0]
    y = lax.conv_general_dilated(x, w[:, None, :], window_strides=(1,),
                                 padding=[(K // 2, K - 1 - K // 2)],
                                 dimension_numbers=('NWC', 'WIO', 'NWC'),
                                 feature_group_count=x.shape[-1])
    return y + b


def headwise(t, w):
    B, T, _ = t.shape
    t = t.reshape(B, T, MLSTM_N_BLOCKS, MLSTM_QKV_BLOCK)
    return jnp.einsum('btni,nij->btnj', t, w).reshape(B, T, MLSTM_INNER)


def mlstm_mixer(a_ctx, a_lat, w_up, conv_w, conv_b, w_qkv, w_gate, b_gate, g_norm, skip, w_down, need_ctx):
    H = MLSTM_HEADS

    def project(a):
        B, T, _ = a.shape
        xm, z = jnp.split(a @ w_up, 2, axis=-1)
        xc = jax.nn.silu(centred_conv(xm, conv_w, conv_b))
        q = headwise(xc, w_qkv[0])
        k = headwise(xc, w_qkv[1])
        v = headwise(xm, w_qkv[2])
        gin = jnp.concatenate([q, k, v], axis=-1)
        heads = lambda t: t.reshape(B, T, H, MLSTM_HD)
        qh, kh, vh = heads(q), heads(k) * MLSTM_HD ** -0.5, heads(v)
        dirs = []
        for d in range(N_DIRS):
            g = (gin @ w_gate[d] + b_gate[d]).astype(jnp.float32)
            dirs.append((qh, kh, vh, g[..., :H], jax.nn.log_sigmoid(g[..., H:])))
        return dirs, xc, z

    def finish(h, xc, z):
        B, T = h.shape[:2]
        hn = head_layernorm(h, g_norm).reshape(B, T, MLSTM_INNER).astype(xc.dtype)
        return ((hn + skip * xc) * jax.nn.silu(z)) @ w_down

    ctx_dirs, xc_ctx, z_ctx = project(a_ctx)
    lat_dirs, xc_lat, z_lat = project(a_lat)
    B = a_lat.shape[0]
    init = (jnp.zeros((B, H, MLSTM_HD, MLSTM_HD), jnp.float32),
            jnp.zeros((B, H, MLSTM_HD), jnp.float32),
            jnp.zeros((B, H), jnp.float32))
    h_ctx, h_lat = bidirectional_scan(mlstm_chunk_scan, ctx_dirs, lat_dirs, init)
    out_ctx = finish(h_ctx, xc_ctx, z_ctx) if need_ctx else None
    return out_ctx, finish(h_lat, xc_lat, z_lat)


def box_mean(x, axis, window):
    length = x.shape[axis]
    pad = [(0, 0)] * x.ndim
    pad[axis] = (1, 0)
    cs = jnp.pad(jnp.cumsum(x.astype(jnp.float32), axis=axis), pad)
    pos = jnp.arange(length)
    lo = jnp.maximum(pos - window // 2, 0)
    hi = jnp.minimum(pos + (window - window // 2), length)
    total = jnp.take(cs, hi, axis=axis) - jnp.take(cs, lo, axis=axis)
    shape = [1] * x.ndim
    shape[axis] = length
    return total / (hi - lo).astype(jnp.float32).reshape(shape)


def pool_mixer(a_ctx, a_lat, w_pool, b_pool, scale, need_ctx):
    B, T, D = a_lat.shape
    rows = T // GRID_W
    grid = a_lat.reshape(B, rows, GRID_W, D)
    lat_parts, ctx_parts = [], []
    for g, win in enumerate(POOL_WINDOWS):
        sl = slice(g * POOL_GROUP, (g + 1) * POOL_GROUP)
        hg = grid[..., sl]
        pooled = box_mean(box_mean(hg, 2, win), 1, win).astype(hg.dtype) - hg
        lat_parts.append(pooled.reshape(B, T, POOL_GROUP) @ w_pool[g] + b_pool[g])
        if need_ctx:
            cg = a_ctx[..., sl]
            pooled_c = box_mean(cg, 1, win).astype(cg.dtype) - cg
            ctx_parts.append(pooled_c @ w_pool[g] + b_pool[g])
    out_lat = jnp.concatenate(lat_parts, axis=-1) * scale
    out_ctx = jnp.concatenate(ctx_parts, axis=-1) * scale if need_ctx else None
    return out_ctx, out_lat


def sqrelu_mlp(h, w1, w2):
    return jnp.square(jax.nn.relu(h @ w1)) @ w2


def setup_inputs(seed: int = 0) -> dict:
    key = jax.random.key(seed)
    keys = iter(jax.random.split(key, 48))

    def normal(shape, scale):
        return jax.random.normal(next(keys), shape, jnp.float32) * scale

    def gain(shape):
        return 1.0 + normal(shape, 0.02)

    n_gla = sum(1 for i in range(DEPTH) if i % N_MIXERS == 0)
    n_mlstm = sum(1 for i in range(DEPTH) if i % N_MIXERS == 1)
    n_pool = sum(1 for i in range(DEPTH) if i % N_MIXERS == 2)
    d = D_MODEL
    gate_i = normal((n_mlstm, N_DIRS, MLSTM_HEADS), 0.1)
    gate_f = jnp.linspace(3.0, 6.0, MLSTM_HEADS, dtype=jnp.float32) + normal((n_mlstm, N_DIRS, MLSTM_HEADS), 0.1)
    return {
        'x': normal((BATCH, SEQ, d), 1.0),
        'c': normal((BATCH, d), 1.0),
        'ctx': normal((BATCH, CTX_LEN, d), 1.0),
        'c_ctx': normal((d,), 1.0),
        'ada_w': normal((DEPTH, d, 6 * d), 0.5 * d ** -0.5),
        'ada_b': normal((DEPTH, 6 * d), 0.01),
        'norm1_g': gain((DEPTH, d)),
        'norm2_g': gain((DEPTH, d)),
        'mlp_w1': normal((DEPTH, d, D_FF), d ** -0.5),
        'mlp_w2': normal((DEPTH, D_FF, d), D_FF ** -0.5),
        'gla_w_in': normal((n_gla, d, GLA_IN), d ** -0.5),
        'gla_w_a1': normal((n_gla, N_DIRS, d, GLA_GATE_RANK), d ** -0.5),
        'gla_w_a2': normal((n_gla, N_DIRS, GLA_GATE_RANK, GLA_DK), GLA_GATE_RANK ** -0.5),
        'gla_b_a': normal((n_gla, N_DIRS, GLA_DK), 0.1),
        'gla_g_head': gain((n_gla, GLA_HEADS, GLA_HV)),
        'gla_w_o': normal((n_gla, GLA_DV, d), GLA_DV ** -0.5),
        'mlstm_w_up': normal((n_mlstm, d, 2 * MLSTM_INNER), d ** -0.5),
        'mlstm_conv_w': normal((n_mlstm, MLSTM_CONV, MLSTM_INNER), MLSTM_CONV ** -0.5),
        'mlstm_conv_b': normal((n_mlstm, MLSTM_INNER), 0.01),
        'mlstm_w_qkv': normal((n_mlstm, 3, MLSTM_N_BLOCKS, MLSTM_QKV_BLOCK, MLSTM_QKV_BLOCK), MLSTM_QKV_BLOCK ** -0.5),
        'mlstm_w_gate': normal((n_mlstm, N_DIRS, 3 * MLSTM_INNER, 2 * MLSTM_HEADS), (3 * MLSTM_INNER) ** -0.5),
        'mlstm_b_gate': jnp.concatenate([gate_i, gate_f], axis=-1),
        'mlstm_g_norm': gain((n_mlstm, MLSTM_HEADS, MLSTM_HD)),
        'mlstm_skip': gain((n_mlstm, MLSTM_INNER)),
        'mlstm_w_down': normal((n_mlstm, MLSTM_INNER, d), MLSTM_INNER ** -0.5),
        'pool_w': normal((n_pool, len(POOL_WINDOWS), POOL_GROUP, POOL_GROUP), POOL_GROUP ** -0.5),
        'pool_b': normal((n_pool, len(POOL_WINDOWS), POOL_GROUP), 0.01),
        'pool_scale': gain((n_pool, d)),
        'final_g': gain((d,)),
    }


def reference(x, c, ctx, c_ctx, ada_w, ada_b, norm1_g, norm2_g, mlp_w1, mlp_w2,
              gla_w_in, gla_w_a1, gla_w_a2, gla_b_a, gla_g_head, gla_w_o,
              mlstm_w_up, mlstm_conv_w, mlstm_conv_b, mlstm_w_qkv, mlstm_w_gate, mlstm_b_gate,
              mlstm_g_norm, mlstm_skip, mlstm_w_down,
              pool_w, pool_b, pool_scale, final_g):
    silu_c = jax.nn.silu(c)
    silu_cc = jax.nn.silu(c_ctx)
    h_lat, h_ctx = x, ctx
    for i in range(DEPTH):
        kind, j = i % N_MIXERS, i // N_MIXERS
        need_ctx = i < DEPTH - 1
        mod_l = jnp.split((silu_c @ ada_w[i] + ada_b[i])[:, None, :], 6, axis=-1)
        mod_c = jnp.split((silu_cc @ ada_w[i] + ada_b[i])[None, None, :], 6, axis=-1)
        a_lat = modulate(rmsnorm(h_lat, norm1_g[i]), mod_l[0], mod_l[1])
        a_ctx = modulate(rmsnorm(h_ctx, norm1_g[i]), mod_c[0], mod_c[1])
        if kind == 0:
            o_ctx, o_lat = gla_mixer(a_ctx, a_lat, gla_w_in[j], gla_w_a1[j], gla_w_a2[j], gla_b_a[j],
                                     gla_g_head[j], gla_w_o[j], need_ctx)
        elif kind == 1:
            o_ctx, o_lat = mlstm_mixer(a_ctx, a_lat, mlstm_w_up[j], mlstm_conv_w[j], mlstm_conv_b[j],
                                       mlstm_w_qkv[j], mlstm_w_gate[j], mlstm_b_gate[j],
                                       mlstm_g_norm[j], mlstm_skip[j], mlstm_w_down[j], need_ctx)
        else:
            o_ctx, o_lat = pool_mixer(a_ctx, a_lat, pool_w[j], pool_b[j], pool_scale[j], need_ctx)
        h_lat = h_lat + mod_l[2] * o_lat
        h_lat = h_lat + mod_l[5] * sqrelu_mlp(modulate(rmsnorm(h_lat, norm2_g[i]), mod_l[3], mod_l[4]),
                                              mlp_w1[i], mlp_w2[i])
        if need_ctx:
            h_ctx = h_ctx + mod_c[2] * o_ctx
            h_ctx = h_ctx + mod_c[5] * sqrelu_mlp(modulate(rmsnorm(h_ctx, norm2_g[i]), mod_c[3], mod_c[4]),
                                                  mlp_w1[i], mlp_w2[i])
    return rmsnorm(h_lat, final_g)
```

```python
import functools

import numpy as np
import jax
import jax.numpy as jnp
from jax import lax
from jax.experimental import pallas as pl
from jax.experimental.pallas import tpu as pltpu

F32 = jnp.float32
BF16 = jnp.bfloat16
HIGHEST = lax.Precision.HIGHEST

D_MODEL = 1024
D_FF = 4 * D_MODEL
DEPTH = 4
N_MIXERS = 3
EPS = 1e-6
GRID_W = 64
ROW_TILE = 256
CHUNK = 64
MOD_ROWS = 16

GLA_HEADS = 4
GLA_DK = D_MODEL // 2
GLA_DV = D_MODEL
GLA_HK = GLA_DK // GLA_HEADS
GLA_HV = GLA_DV // GLA_HEADS
GLA_IN = 2 * GLA_DK + 2 * GLA_DV
GLA_RANK = 16
GLA_TAU = 16.0

ML_HEADS = 4
ML_INNER = 2 * D_MODEL
ML_HD = ML_INNER // ML_HEADS
ML_BLOCK = 4
ML_CONV = 4
ML_BD = 256
HALO = 8

POOL_WINDOWS = (2, 4, 8, 16)
POOL_GROUP = D_MODEL // len(POOL_WINDOWS)

VMEM_LIMIT = 56 * 1024 * 1024

NT_DIMS = (((1,), (1,)), ((), ()))
TN_DIMS = (((0,), (0,)), ((), ()))


def _params(*sem):
    return pltpu.CompilerParams(dimension_semantics=sem, vmem_limit_bytes=VMEM_LIMIT)


def _silu(x):
    return x * jax.nn.sigmoid(x)


def _log_sigmoid(z):
    return jnp.minimum(z, 0.0) - jnp.log1p(jnp.exp(-jnp.abs(z)))


def _rms(x):
    return x * lax.rsqrt(jnp.mean(x * x, axis=-1, keepdims=True) + EPS)


def _norm_mod(h, g, shift, scale):
    return (_rms(h) * g) * (1.0 + scale) + shift


def _dot(a, b):
    return jnp.dot(a, b, preferred_element_type=F32)


def _dot_f32(a, b):
    return jnp.dot(a, b, precision=HIGHEST, preferred_element_type=F32)


def _mod_kernel(cc_ref, w_ref, b_ref, o_ref):
    s = _silu(cc_ref[...])
    o_ref[0] = _dot_f32(s, w_ref[0]) + b_ref[0]


def _mod_table(cc, ada_w, ada_b):
    depth, d, n = ada_w.shape
    tn = 1536
    return pl.pallas_call(
        _mod_kernel,
        grid=(depth, n // tn),
        in_specs=[
            pl.BlockSpec((MOD_ROWS, d), lambda i, j: (0, 0)),
            pl.BlockSpec((1, d, tn), lambda i, j: (i, 0, j)),
            pl.BlockSpec((1, 1, tn), lambda i, j: (i, 0, j)),
        ],
        out_specs=pl.BlockSpec((1, MOD_ROWS, tn), lambda i, j: (i, 0, j)),
        out_shape=jax.ShapeDtypeStruct((depth, MOD_ROWS, n), F32),
        compiler_params=_params("arbitrary", "arbitrary"),
        name="mod_table",
    )(cc, ada_w, ada_b.reshape(depth, 1, n))


def _row_spec(width):
    return pl.BlockSpec((ROW_TILE, width), lambda t: (t, 0))


def _full_spec(shape):
    nd = len(shape)
    return pl.BlockSpec(shape, lambda t: (0,) * nd)


def _mod_spec(tiles_per_batch, ctx_row):
    def index(t):
        return (jnp.where(t % tiles_per_batch == 0, ctx_row, t // tiles_per_batch), 0, 0)
    return pl.BlockSpec((1, 6, D_MODEL), index)


def _gla_proj_kernel(h_ref, mod_ref, g_ref, win_ref, wa1_ref, wa2_ref, ba_ref,
                     q_ref, k_ref, v_ref, r_ref, la_ref):
    a = _norm_mod(h_ref[...], g_ref[...], mod_ref[0, 0:1, :], mod_ref[0, 1:2, :]).astype(BF16)
    y = _dot(a, win_ref[...])
    q_ref[...] = y[:, :GLA_DK] * (GLA_HK ** -0.5)
    k_ref[...] = y[:, GLA_DK:2 * GLA_DK]
    v_ref[...] = y[:, 2 * GLA_DK:2 * GLA_DK + GLA_DV].astype(BF16)
    r_ref[...] = y[:, 2 * GLA_DK + GLA_DV:]
    t = _dot(a, wa1_ref[...]).astype(BF16)
    z = _dot(t, wa2_ref[...]) + ba_ref[...]
    la_ref[...] = _log_sigmoid(z) * (1.0 / GLA_TAU)


def _gla_fin_kernel(o_ref, r_ref, h_ref, mod_ref, gh_ref, wo_ref, out_ref):
    o = o_ref[0] + o_ref[1]
    parts = [_rms(o[:, i * GLA_HV:(i + 1) * GLA_HV]) for i in range(GLA_HEADS)]
    on = jnp.concatenate(parts, axis=-1) * gh_ref[...]
    y = (on * _silu(r_ref[...])).astype(BF16)
    out_ref[...] = h_ref[...] + mod_ref[0, 2:3, :] * _dot(y, wo_ref[...])


def _mlp_kernel(h_ref, mod_ref, g_ref, w1_ref, w2_ref, fg_ref, out_ref, *, final):
    h = h_ref[...]
    a = _norm_mod(h, g_ref[...], mod_ref[0, 3:4, :], mod_ref[0, 4:5, :]).astype(BF16)
    u = jnp.square(jnp.maximum(_dot(a, w1_ref[...]), 0.0)).astype(BF16)
    hn = h + mod_ref[0, 5:6, :] * _dot(u, w2_ref[...])
    if final:
        hn = _rms(hn) * fg_ref[...]
    out_ref[...] = hn


def _ml_proj_kernel(h_ref, mod_ref, g_ref, wup_ref, xm_ref, z_ref):
    a = _norm_mod(h_ref[...], g_ref[...], mod_ref[0, 0:1, :], mod_ref[0, 1:2, :]).astype(BF16)
    y = _dot(a, wup_ref[...])
    xm_ref[...] = y[:, :ML_INNER]
    z_ref[...] = y[:, ML_INNER:]


def _ml_fin_kernel(hs_ref, xc_ref, z_ref, h_ref, mod_ref, gn_ref, skip_ref, wd_ref, out_ref):
    hs = hs_ref[0] + hs_ref[1]
    parts = []
    for i in range(ML_HEADS):
        o = hs[:, i * ML_HD:(i + 1) * ML_HD]
        parts.append(_rms(o - jnp.mean(o, axis=-1, keepdims=True)))
    hn = jnp.concatenate(parts, axis=-1) * gn_ref[...]
    y = ((hn + skip_ref[...] * xc_ref[...]) * _silu(z_ref[...])).astype(BF16)
    out_ref[...] = h_ref[...] + mod_ref[0, 2:3, :] * _dot(y, wd_ref[...])


def _norm_mod_kernel(h_ref, mod_ref, g_ref, a_ref):
    a_ref[...] = _norm_mod(h_ref[...], g_ref[...], mod_ref[0, 0:1, :], mod_ref[0, 1:2, :])


def _row_call(kernel, n_rows, tiles_per_batch, ctx_row, ins, in_specs, outs, name):
    return pl.pallas_call(
        kernel,
        grid=(n_rows // ROW_TILE,),
        in_specs=in_specs,
        out_specs=[_row_spec(w) for w, _ in outs],
        out_shape=[jax.ShapeDtypeStruct((n_rows, w), dt) for w, dt in outs],
        compiler_params=_params("arbitrary"),
        name=name,
    )(*ins)


def _scan_block(n_blocks):
    def block(d, s):
        return jnp.where(d == 0, s, jnp.where(s == 0, 0, n_blocks - s))
    return block


def _tri_pair(length):
    lower = np.tril(np.ones((length, length), np.float32))
    return jnp.asarray(np.stack([lower, lower.T]))


def _gla_scan_kernel(q_ref, k_ref, v_ref, la_ref, tri_ref, o_ref, st_ref):
    d = pl.program_id(1)
    n_chunks = ROW_TILE // CHUNK

    @pl.when(pl.program_id(2) == 0)
    def _():
        st_ref[...] = jnp.zeros_like(st_ref)

    tri = tri_ref[0]
    keep = tri > 0.0

    def chunk(j, carry):
        c = jnp.where(d == 0, j, n_chunks - 1 - j)
        rows = pl.ds(pl.multiple_of(c * CHUNK, CHUNK), CHUNK)
        for i in range(GLA_HEADS):
            kc = slice(i * GLA_HK, (i + 1) * GLA_HK)
            vc = slice(i * GLA_HV, (i + 1) * GLA_HV)
            g = la_ref[0, rows, kc]
            b = _dot_f32(tri, g)
            b_tot = jnp.sum(g, axis=0, keepdims=True)
            q = q_ref[0, rows, kc]
            k = k_ref[0, rows, kc]
            v = v_ref[0, rows, vc]
            q_dec = (q * jnp.exp(b)).astype(BF16)
            k_inv = (k * jnp.exp(-b)).astype(BF16)
            k_rem = (k * jnp.exp(b_tot - b)).astype(BF16)
            att = lax.dot_general(q_dec, k_inv, NT_DIMS, preferred_element_type=F32)
            att = jnp.where(keep, att, 0.0).astype(BF16)
            st = st_ref[i]
            o = _dot(att, v) + lax.dot_general(q_dec, st.astype(BF16), NT_DIMS,
                                               preferred_element_type=F32)
            o_ref[0, 0, rows, vc] = o
            st_ref[i] = st * jnp.exp(b_tot) + lax.dot_general(v, k_rem, TN_DIMS,
                                                              preferred_element_type=F32)
        return carry

    lax.fori_loop(0, n_chunks, chunk, 0)


def _gla_scan(q, k, v, la, batch, tokens):
    n_blocks = tokens // ROW_TILE
    block = _scan_block(n_blocks)
    q3, k3 = q.reshape(batch, tokens, GLA_DK), k.reshape(batch, tokens, GLA_DK)
    v3 = v.reshape(batch, tokens, GLA_DV)
    la3 = la.reshape(batch, tokens, 2 * GLA_DK)
    o = pl.pallas_call(
        _gla_scan_kernel,
        grid=(batch, 2, n_blocks),
        in_specs=[
            pl.BlockSpec((1, ROW_TILE, GLA_DK), lambda b, d, s: (b, block(d, s), 0)),
            pl.BlockSpec((1, ROW_TILE, GLA_DK), lambda b, d, s: (b, block(d, s), 0)),
            pl.BlockSpec((1, ROW_TILE, GLA_DV), lambda b, d, s: (b, block(d, s), 0)),
            pl.BlockSpec((1, ROW_TILE, GLA_DK), lambda b, d, s: (b, block(d, s), d)),
            pl.BlockSpec((1, CHUNK, CHUNK), lambda b, d, s: (d, 0, 0)),
        ],
        out_specs=pl.BlockSpec((1, 1, ROW_TILE, GLA_DV), lambda b, d, s: (d, b, block(d, s), 0)),
        out_shape=jax.ShapeDtypeStruct((2, batch, tokens, GLA_DV), F32),
        scratch_shapes=[pltpu.VMEM((GLA_HEADS, GLA_HV, GLA_HK), F32)],
        compiler_params=_params("arbitrary", "arbitrary", "arbitrary"),
        name="gla_scan",
    )(q3, k3, v3, la3, _tri_pair(CHUNK))
    return o.reshape(2, batch * tokens, GLA_DV)


def _ml_scan_kernel(q_ref, k_ref, v_ref, g_ref, tri_ref, o_ref, c_ref, n_ref, m_ref):
    d = pl.program_id(1)
    n_chunks = ROW_TILE // CHUNK
    H = ML_HEADS

    @pl.when(pl.program_id(2) == 0)
    def _():
        c_ref[...] = jnp.zeros_like(c_ref)
        n_ref[...] = jnp.zeros_like(n_ref)
        m_ref[...] = jnp.zeros_like(m_ref)

    tri = tri_ref[0]
    keep = tri > 0.0

    def chunk(j, carry):
        c = jnp.where(d == 0, j, n_chunks - 1 - j)
        rows = pl.ds(pl.multiple_of(c * CHUNK, CHUNK), CHUNK)
        g = g_ref[0, rows, :]
        cum = _dot_f32(tri, g)
        tot = jnp.sum(g, axis=0, keepdims=True)
        g_t = g.T
        cum_t = cum.T
        for i in range(H):
            hc = slice(i * ML_HD, (i + 1) * ML_HD)
            i_col, b_col = g[:, i:i + 1], cum[:, H + i:H + i + 1]
            i_row, b_row = g_t[i:i + 1, :], cum_t[H + i:H + i + 1, :]
            b_last = tot[:, H + i:H + i + 1]
            q = q_ref[0, rows, hc]
            k = k_ref[0, rows, hc]
            v = v_ref[0, rows, hc]
            m_prev = m_ref[i, 0:1, 0:1]
            d_mat = jnp.where(keep, b_col - b_row + i_row, -jnp.inf)
            inter = b_col + m_prev
            m_row = jnp.maximum(jnp.max(d_mat, axis=-1, keepdims=True), inter)
            qk = lax.dot_general(q, k, NT_DIMS, preferred_element_type=F32)
            w_qk = qk * jnp.exp(d_mat - m_row)
            w_inter = jnp.exp(inter - m_row)
            c_bar = c_ref[i]
            n_bar = n_ref[i]
            num = _dot(w_qk.astype(BF16), v) + w_inter * _dot(q, c_bar.astype(BF16))
            den = (jnp.sum(w_qk, axis=-1, keepdims=True)
                   + w_inter * jnp.sum(q.astype(F32) * n_bar, axis=-1, keepdims=True))
            o_ref[0, 0, rows, hc] = num / jnp.maximum(jnp.abs(den), jnp.exp(-m_row))
            carry_log = b_last + m_prev
            m_new = jnp.maximum(carry_log, jnp.max(b_last - b_row + i_row, axis=-1, keepdims=True))
            w_tok = jnp.exp(b_last - b_col + i_col - m_new)
            w_carry = jnp.exp(carry_log - m_new)
            k_w = k.astype(F32) * w_tok
            c_ref[i] = w_carry * c_bar + lax.dot_general(k_w.astype(BF16), v, TN_DIMS,
                                                         preferred_element_type=F32)
            n_ref[i] = w_carry * n_bar + jnp.sum(k_w, axis=0, keepdims=True)
            m_ref[i] = jnp.broadcast_to(m_new, m_ref.shape[1:])
        return carry

    lax.fori_loop(0, n_chunks, chunk, 0)


def _ml_scan(q, k, v, gates, batch, tokens):
    n_blocks = tokens // ROW_TILE
    block = _scan_block(n_blocks)
    shape3 = (batch, tokens, ML_INNER)
    o = pl.pallas_call(
        _ml_scan_kernel,
        grid=(batch, 2, n_blocks),
        in_specs=[
            pl.BlockSpec((1, ROW_TILE, ML_INNER), lambda b, d, s: (b, block(d, s), 0)),
            pl.BlockSpec((1, ROW_TILE, ML_INNER), lambda b, d, s: (b, block(d, s), 0)),
            pl.BlockSpec((1, ROW_TILE, ML_INNER), lambda b, d, s: (b, block(d, s), 0)),
            pl.BlockSpec((1, ROW_TILE, 128), lambda b, d, s: (b, block(d, s), d)),
            pl.BlockSpec((1, CHUNK, CHUNK), lambda b, d, s: (d, 0, 0)),
        ],
        out_specs=pl.BlockSpec((1, 1, ROW_TILE, ML_INNER), lambda b, d, s: (d, b, block(d, s), 0)),
        out_shape=jax.ShapeDtypeStruct((2, batch, tokens, ML_INNER), F32),
        scratch_shapes=[pltpu.VMEM((ML_HEADS, ML_HD, ML_HD), F32),
                        pltpu.VMEM((ML_HEADS, 1, ML_HD), F32),
                        pltpu.VMEM((ML_HEADS, 8, 128), F32)],
        compiler_params=_params("arbitrary", "arbitrary", "arbitrary"),
        name="mlstm_scan",
    )(q.reshape(shape3), k.reshape(shape3), v.reshape(shape3),
      gates.reshape(batch, tokens, 256), _tri_pair(CHUNK))
    return o.reshape(2, batch * tokens, ML_INNER)


def _ml_conv_kernel(x_ref, xp_ref, xn_ref, cw_ref, cb_ref, wbd_ref, wg_ref, bg_ref,
                    xc_ref, q_ref, k_ref, v_ref, g_ref, ext_ref, *, n_blocks):
    t = pl.program_id(1)
    prev_ok = t >= 2
    next_ok = jnp.logical_and(t >= 1, t < n_blocks - 1)
    ext_ref[0:HALO, :] = jnp.where(prev_ok, xp_ref[0], 0.0)
    ext_ref[HALO:HALO + ROW_TILE, :] = x_ref[0]
    ext_ref[HALO + ROW_TILE:, :] = jnp.where(next_ok, xn_ref[0], 0.0)
    gates = jnp.zeros((ROW_TILE, 256), F32)
    for cb in range(ML_INNER // ML_BD):
        cols = slice(cb * ML_BD, (cb + 1) * ML_BD)
        y = cb_ref[:, cols]
        for tap in range(ML_CONV):
            off = HALO + tap - ML_CONV // 2
            y = y + cw_ref[tap:tap + 1, cols] * ext_ref[off:off + ROW_TILE, cols]
        xc = _silu(y)
        xc_ref[0, :, cols] = xc
        xc_b = xc.astype(BF16)
        xm_b = ext_ref[HALO:HALO + ROW_TILE, cols].astype(BF16)
        q = _dot(xc_b, wbd_ref[0, cb])
        k = _dot(xc_b, wbd_ref[1, cb])
        v = _dot(xm_b, wbd_ref[2, cb])
        q_b, k_b, v_b = q.astype(BF16), k.astype(BF16), v.astype(BF16)
        q_ref[0, :, cols] = q_b
        k_ref[0, :, cols] = (k * (ML_HD ** -0.5)).astype(BF16)
        v_ref[0, :, cols] = v_b
        for j, part in enumerate((q_b, k_b, v_b)):
            gates = gates + _dot(part, wg_ref[j * ML_INNER + cb * ML_BD:j * ML_INNER + (cb + 1) * ML_BD, :])
    gates = gates + bg_ref[...]
    lane = lax.broadcasted_iota(jnp.int32, gates.shape, 1) % 128
    is_forget = jnp.logical_and(lane >= ML_HEADS, lane < 2 * ML_HEADS)
    g_ref[0] = jnp.where(is_forget, _log_sigmoid(gates), gates)


def _ml_conv(xm, conv_w, conv_b, wbd, wg, bg, batch, tokens):
    n_blocks = tokens // ROW_TILE
    per = ROW_TILE // HALO
    last = tokens // HALO - 1
    xm3 = xm.reshape(batch, tokens, ML_INNER)
    blk = lambda b, t: (b, t, 0)
    full = lambda shape: pl.BlockSpec(shape, lambda b, t: (0,) * len(shape))
    outs = pl.pallas_call(
        functools.partial(_ml_conv_kernel, n_blocks=n_blocks),
        grid=(batch, n_blocks),
        in_specs=[
            pl.BlockSpec((1, ROW_TILE, ML_INNER), blk),
            pl.BlockSpec((1, HALO, ML_INNER), lambda b, t: (b, jnp.maximum(t * per - 1, 0), 0)),
            pl.BlockSpec((1, HALO, ML_INNER), lambda b, t: (b, jnp.minimum((t + 1) * per, last), 0)),
            full(conv_w.shape), full(conv_b.shape), full(wbd.shape), full(wg.shape), full(bg.shape),
        ],
        out_specs=[pl.BlockSpec((1, ROW_TILE, ML_INNER), blk)] * 4
        + [pl.BlockSpec((1, ROW_TILE, 256), blk)],
        out_shape=[jax.ShapeDtypeStruct((batch, tokens, ML_INNER), F32)]
        + [jax.ShapeDtypeStruct((batch, tokens, ML_INNER), BF16)] * 3
        + [jax.ShapeDtypeStruct((batch, tokens, 256), F32)],
        scratch_shapes=[pltpu.VMEM((ROW_TILE + 2 * HALO, ML_INNER), F32)],
        compiler_params=_params("arbitrary", "arbitrary"),
        name="mlstm_conv",
    )(xm3, xm3, xm3, conv_w, conv_b, wbd, wg, bg)
    n = batch * tokens
    xc, q, k, v, g = outs
    return (xc.reshape(n, ML_INNER), q.reshape(n, ML_INNER), k.reshape(n, ML_INNER),
            v.reshape(n, ML_INNER), g.reshape(n, 256))


def _pool_kernel(a_ref, h_ref, modl_ref, modc_ref, band_ref, wp_ref, bp_ref, sc_ref,
                 o_ref, y_ref, *, n_rows):
    group = pl.program_id(1)
    ctx = ROW_TILE
    wp = wp_ref[0]
    bp = bp_ref[0]
    sc = sc_ref[...]
    gate_l = modl_ref[0, 2:3, :]
    gate_c = modc_ref[0, 2:3, :]

    def mix(delta, res, gate):
        return res + gate * ((_dot(delta.astype(BF16), wp) + bp) * sc)

    def body(win):
        lo, hi = win // 2, win - 1 - win // 2

        def count(pos, length):
            return (jnp.minimum(pos + hi, length - 1) - jnp.maximum(pos - lo, 0) + 1).astype(F32)

        x = a_ref[0, 0:ctx, :]
        tpos = lax.broadcasted_iota(jnp.int32, (ctx, POOL_GROUP), 0)
        pooled = _dot_f32(band_ref[0, 0], x) / count(tpos, ctx)
        o_ref[0, 0:ctx, :] = mix(pooled - x, h_ref[0, 0:ctx, :], gate_c)

        band = band_ref[0, 1]

        def width_sum(tb, carry):
            src = pl.ds(pl.multiple_of(ctx + tb * ROW_TILE, ROW_TILE), ROW_TILE)
            dst = pl.ds(pl.multiple_of(tb * ROW_TILE, ROW_TILE), ROW_TILE)
            y_ref[dst, :] = _dot_f32(band, a_ref[0, src, :])
            return carry

        lax.fori_loop(0, n_rows * GRID_W // ROW_TILE, width_sum, 0)

        cpos = lax.broadcasted_iota(jnp.int32, (GRID_W, POOL_GROUP), 0)
        cnt_w = count(cpos, GRID_W)

        def height_sum(r, carry):
            acc = jnp.zeros((GRID_W, POOL_GROUP), F32)
            for dlt in range(-lo, hi + 1):
                rr = r + dlt
                valid = jnp.logical_and(rr >= 0, rr < n_rows)
                rc = jnp.clip(rr, 0, n_rows - 1)
                yv = y_ref[pl.ds(pl.multiple_of(rc * GRID_W, GRID_W), GRID_W), :]
                acc = acc + jnp.where(valid, yv, 0.0)
            cnt_h = count(r, n_rows)
            rows = pl.ds(pl.multiple_of(ctx + r * GRID_W, GRID_W), GRID_W)
            pooled = acc / (cnt_w * cnt_h)
            o_ref[0, rows, :] = mix(pooled - a_ref[0, rows, :], h_ref[0, rows, :], gate_l)
            return carry

        lax.fori_loop(0, n_rows, height_sum, 0)

    for gi, win in enumerate(POOL_WINDOWS):
        pl.when(group == gi)(functools.partial(body, win))


def _pool_bands():
    bands = np.zeros((len(POOL_WINDOWS), 2, ROW_TILE, ROW_TILE), np.float32)
    t = np.arange(ROW_TILE)
    for gi, win in enumerate(POOL_WINDOWS):
        lo, hi = win // 2, win - 1 - win // 2
        near = (t[None, :] - t[:, None] >= -lo) & (t[None, :] - t[:, None] <= hi)
        same_row = (t[None, :] // GRID_W) == (t[:, None] // GRID_W)
        bands[gi, 0] = near
        bands[gi, 1] = near & same_row
    return jnp.asarray(bands)


def _pool_mix(a, h, mod, w_pool, b_pool, scale, batch, tokens, ctx_row):
    n_groups = len(POOL_WINDOWS)
    lat = tokens - ROW_TILE
    a3, h3 = a.reshape(batch, tokens, D_MODEL), h.reshape(batch, tokens, D_MODEL)
    slab = pl.BlockSpec((1, tokens, POOL_GROUP), lambda b, g: (b, 0, g))
    out = pl.pallas_call(
        functools.partial(_pool_kernel, n_rows=lat // GRID_W),
        grid=(batch, n_groups),
        in_specs=[
            slab, slab,
            pl.BlockSpec((1, 6, POOL_GROUP), lambda b, g: (b, 0, g)),
            pl.BlockSpec((1, 6, POOL_GROUP), lambda b, g: (ctx_row, 0, g)),
            pl.BlockSpec((1, 2, ROW_TILE, ROW_TILE), lambda b, g: (g, 0, 0, 0)),
            pl.BlockSpec((1, POOL_GROUP, POOL_GROUP), lambda b, g: (g, 0, 0)),
            pl.BlockSpec((1, 1, POOL_GROUP), lambda b, g: (g, 0, 0)),
            pl.BlockSpec((1, POOL_GROUP), lambda b, g: (0, g)),
        ],
        out_specs=slab,
        out_shape=jax.ShapeDtypeStruct((batch, tokens, D_MODEL), F32),
        scratch_shapes=[pltpu.VMEM((lat, POOL_GROUP), F32)],
        compiler_params=_params("arbitrary", "arbitrary"),
        name="pool_mix",
    )(a3, h3, mod, mod, _pool_bands(), w_pool.astype(BF16),
      b_pool.reshape(n_groups, 1, POOL_GROUP), scale.reshape(1, D_MODEL))
    return out.reshape(batch * tokens, D_MODEL)


def _block_diag_tiles(w):
    per = ML_BD // ML_BLOCK
    w = w.reshape(-1, per, ML_BLOCK, ML_BLOCK)
    eye = jnp.eye(per, dtype=w.dtype)
    t = w[:, :, :, None, :] * eye[None, :, None, :, None]
    return t.reshape(-1, ML_BD, ML_BD)


def _gla_gate_weights(w_a1, w_a2, b_a):
    wa1 = jnp.zeros((D_MODEL, 128), F32)
    wa2 = jnp.zeros((128, 2 * GLA_DK), F32)
    for d in range(2):
        wa1 = wa1.at[:, d * GLA_RANK:(d + 1) * GLA_RANK].set(w_a1[d])
        wa2 = wa2.at[d * GLA_RANK:(d + 1) * GLA_RANK, d * GLA_DK:(d + 1) * GLA_DK].set(w_a2[d])
    return wa1.astype(BF16), wa2.astype(BF16), b_a.reshape(1, 2 * GLA_DK)


def _ml_gate_weights(w_gate, b_gate):
    wg = jnp.zeros((3 * ML_INNER, 256), F32)
    bg = jnp.zeros((1, 256), F32)
    for d in range(2):
        wg = wg.at[:, d * 128:d * 128 + 2 * ML_HEADS].set(w_gate[d])
        bg = bg.at[0, d * 128:d * 128 + 2 * ML_HEADS].set(b_gate[d])
    return wg.astype(BF16), bg


def kernel(x, c, ctx, c_ctx, ada_w, ada_b, norm1_g, norm2_g, mlp_w1, mlp_w2, gla_w_in, gla_w_a1, gla_w_a2, gla_b_a, gla_g_head, gla_w_o, mlstm_w_up, mlstm_conv_w, mlstm_conv_b, mlstm_w_qkv, mlstm_w_gate, mlstm_b_gate, mlstm_g_norm, mlstm_skip, mlstm_w_down, pool_w, pool_b, pool_scale, final_g):
    batch, seq, d = x.shape
    ctx_len = ctx.shape[1]
    depth = ada_w.shape[0]
    assert d == D_MODEL and ctx_len == ROW_TILE and seq % ROW_TILE == 0 and batch < MOD_ROWS
    tokens = ctx_len + seq
    n = batch * tokens
    tpb = tokens // ROW_TILE
    ctx_row = batch

    cc = jnp.zeros((MOD_ROWS, d), F32).at[:batch].set(c).at[ctx_row].set(c_ctx)
    mod_all = _mod_table(cc, ada_w, ada_b).reshape(depth, MOD_ROWS, 6, d)

    h = jnp.concatenate([ctx, x], axis=1).reshape(n, d)
    mod_spec = _mod_spec(tpb, ctx_row)
    vec = lambda w: _full_spec((1, w))
    row_call = functools.partial(_row_call, n_rows=n, tiles_per_batch=tpb, ctx_row=ctx_row)

    for i in range(depth):
        kind, j = i % N_MIXERS, i // N_MIXERS
        mod = mod_all[i]
        g1 = norm1_g[i].reshape(1, d)
        if kind == 0:
            w_in = gla_w_in[j].astype(BF16)
            wa1, wa2, ba = _gla_gate_weights(gla_w_a1[j], gla_w_a2[j], gla_b_a[j])
            q, k, v, r, la = row_call(
                _gla_proj_kernel,
                ins=(h, mod, g1, w_in, wa1, wa2, ba),
                in_specs=[_row_spec(d), mod_spec, vec(d), _full_spec(w_in.shape),
                          _full_spec(wa1.shape), _full_spec(wa2.shape), vec(2 * GLA_DK)],
                outs=[(GLA_DK, F32), (GLA_DK, F32), (GLA_DV, BF16), (GLA_DV, F32), (2 * GLA_DK, F32)],
                name="gla_proj")
            o = _gla_scan(q, k, v, la, batch, tokens)
            w_o = gla_w_o[j].astype(BF16)
            (h,) = row_call(
                _gla_fin_kernel,
                ins=(o, r, h, mod, gla_g_head[j].reshape(1, GLA_DV), w_o),
                in_specs=[pl.BlockSpec((2, ROW_TILE, GLA_DV), lambda t: (0, t, 0)), _row_spec(GLA_DV),
                          _row_spec(d), mod_spec, vec(GLA_DV), _full_spec(w_o.shape)],
                outs=[(d, F32)], name="gla_fin")
        elif kind == 1:
            w_up = mlstm_w_up[j].astype(BF16)
            xm, z = row_call(
                _ml_proj_kernel,
                ins=(h, mod, g1, w_up),
                in_specs=[_row_spec(d), mod_spec, vec(d), _full_spec(w_up.shape)],
                outs=[(ML_INNER, F32), (ML_INNER, F32)], name="mlstm_proj")
            wbd = jnp.stack([_block_diag_tiles(mlstm_w_qkv[j, p]) for p in range(3)]).astype(BF16)
            wg, bg = _ml_gate_weights(mlstm_w_gate[j], mlstm_b_gate[j])
            xc, q, k, v, gates = _ml_conv(xm, mlstm_conv_w[j], mlstm_conv_b[j].reshape(1, ML_INNER),
                                          wbd, wg, bg, batch, tokens)
            hs = _ml_scan(q, k, v, gates, batch, tokens)
            w_down = mlstm_w_down[j].astype(BF16)
            (h,) = row_call(
                _ml_fin_kernel,
                ins=(hs, xc, z, h, mod, mlstm_g_norm[j].reshape(1, ML_INNER),
                     mlstm_skip[j].reshape(1, ML_INNER), w_down),
                in_specs=[pl.BlockSpec((2, ROW_TILE, ML_INNER), lambda t: (0, t, 0)),
                          _row_spec(ML_INNER), _row_spec(ML_INNER), _row_spec(d), mod_spec,
                          vec(ML_INNER), vec(ML_INNER), _full_spec(w_down.shape)],
                outs=[(d, F32)], name="mlstm_fin")
        else:
            (a,) = row_call(
                _norm_mod_kernel, ins=(h, mod, g1),
                in_specs=[_row_spec(d), mod_spec, vec(d)], outs=[(d, F32)], name="pool_norm")
            h = _pool_mix(a, h, mod, pool_w[j], pool_b[j], pool_scale[j], batch, tokens, ctx_row)

        w1, w2 = mlp_w1[i].astype(BF16), mlp_w2[i].astype(BF16)
        final = i == depth - 1
        (h,) = row_call(
            functools.partial(_mlp_kernel, final=final),
            ins=(h, mod, norm2_g[i].reshape(1, d), w1, w2, final_g.reshape(1, d)),
            in_specs=[_row_spec(d), mod_spec, vec(d), _full_spec(w1.shape), _full_spec(w2.shape), vec(d)],
            outs=[(d, F32)], name="mlp")

    return h.reshape(batch, tokens, d)[:, ctx_len:, :]
```

```python
import functools

import numpy as np
import jax
import jax.numpy as jnp
from jax import lax
from jax.experimental import pallas as pl
from jax.experimental.pallas import tpu as pltpu

F32 = jnp.float32
BF16 = jnp.bfloat16
HIGHEST = lax.Precision.HIGHEST

D_MODEL = 1024
D_FF = 4 * D_MODEL
DEPTH = 4
N_MIXERS = 3
EPS = 1e-6
GRID_W = 64
ROW_TILE = 256
CHUNK = 64
MOD_ROWS = 16

GLA_HEADS = 4
GLA_DK = D_MODEL // 2
GLA_DV = D_MODEL
GLA_HK = GLA_DK // GLA_HEADS
GLA_HV = GLA_DV // GLA_HEADS
GLA_IN = 2 * GLA_DK + 2 * GLA_DV
GLA_RANK = 16
GLA_TAU = 16.0

ML_HEADS = 4
ML_INNER = 2 * D_MODEL
ML_HD = ML_INNER // ML_HEADS
ML_BLOCK = 4
ML_CONV = 4
ML_BD = 256
HALO = 8

POOL_WINDOWS = (2, 4, 8, 16)
POOL_GROUP = D_MODEL // len(POOL_WINDOWS)

VMEM_LIMIT = 56 * 1024 * 1024

NT_DIMS = (((1,), (1,)), ((), ()))
TN_DIMS = (((0,), (0,)), ((), ()))


def _params(*sem):
    return pltpu.CompilerParams(dimension_semantics=sem, vmem_limit_bytes=VMEM_LIMIT)


def _silu(x):
    return x * jax.nn.sigmoid(x)


def _log_sigmoid(z):
    return jnp.minimum(z, 0.0) - jnp.log1p(jnp.exp(-jnp.abs(z)))


def _rms(x):
    return x * lax.rsqrt(jnp.mean(x * x, axis=-1, keepdims=True) + EPS)


def _norm_mod(h, g, shift, scale):
    return (_rms(h) * g) * (1.0 + scale) + shift


def _dot(a, b):
    return jnp.dot(a, b, preferred_element_type=F32)


def _dot_f32(a, b):
    return jnp.dot(a, b, precision=HIGHEST, preferred_element_type=F32)


def _mod_kernel(cc_ref, w_ref, b_ref, o_ref):
    s = _silu(cc_ref[...])
    o_ref[0] = _dot_f32(s, w_ref[0]) + b_ref[0]


def _mod_table(cc, ada_w, ada_b):
    depth, d, n = ada_w.shape
    tn = 1536
    return pl.pallas_call(
        _mod_kernel,
        grid=(depth, n // tn),
        in_specs=[
            pl.BlockSpec((MOD_ROWS, d), lambda i, j: (0, 0)),
            pl.BlockSpec((1, d, tn), lambda i, j: (i, 0, j)),
            pl.BlockSpec((1, 1, tn), lambda i, j: (i, 0, j)),
        ],
        out_specs=pl.BlockSpec((1, MOD_ROWS, tn), lambda i, j: (i, 0, j)),
        out_shape=jax.ShapeDtypeStruct((depth, MOD_ROWS, n), F32),
        compiler_params=_params("arbitrary", "arbitrary"),
        name="mod_table",
    )(cc, ada_w, ada_b.reshape(depth, 1, n))


def _row_spec(width):
    return pl.BlockSpec((ROW_TILE, width), lambda t: (t, 0))


def _full_spec(shape):
    nd = len(shape)
    return pl.BlockSpec(shape, lambda t: (0,) * nd)


def _mod_spec(tiles_per_batch, ctx_row):
    def index(t):
        return (jnp.where(t % tiles_per_batch == 0, ctx_row, t // tiles_per_batch), 0, 0)
    return pl.BlockSpec((1, 6, D_MODEL), index)


def _gla_proj_kernel(h_ref, mod_ref, g_ref, win_ref, wa1_ref, wa2_ref, ba_ref,
                     q_ref, k_ref, v_ref, r_ref, la_ref):
    a = _norm_mod(h_ref[...], g_ref[...], mod_ref[0, 0:1, :], mod_ref[0, 1:2, :]).astype(BF16)
    y = _dot(a, win_ref[...])
    q_ref[...] = y[:, :GLA_DK] * (GLA_HK ** -0.5)
    k_ref[...] = y[:, GLA_DK:2 * GLA_DK]
    v_ref[...] = y[:, 2 * GLA_DK:2 * GLA_DK + GLA_DV].astype(BF16)
    r_ref[...] = y[:, 2 * GLA_DK + GLA_DV:]
    t = _dot(a, wa1_ref[...]).astype(BF16)
    z = _dot(t, wa2_ref[...]) + ba_ref[...]
    la_ref[...] = _log_sigmoid(z) * (1.0 / GLA_TAU)


def _gla_fin_kernel(o_ref, r_ref, h_ref, mod_ref, gh_ref, wo_ref, out_ref):
    o = o_ref[0] + o_ref[1]
    parts = [_rms(o[:, i * GLA_HV:(i + 1) * GLA_HV]) for i in range(GLA_HEADS)]
    on = jnp.concatenate(parts, axis=-1) * gh_ref[...]
    y = (on * _silu(r_ref[...])).astype(BF16)
    out_ref[...] = h_ref[...] + mod_ref[0, 2:3, :] * _dot(y, wo_ref[...])


def _mlp_kernel(h_ref, mod_ref, g_ref, w1_ref, w2_ref, fg_ref, out_ref, *, final):
    h = h_ref[...]
    a = _norm_mod(h, g_ref[...], mod_ref[0, 3:4, :], mod_ref[0, 4:5, :]).astype(BF16)
    u = jnp.square(jnp.maximum(_dot(a, w1_ref[...]), 0.0)).astype(BF16)
    hn = h + mod_ref[0, 5:6, :] * _dot(u, w2_ref[...])
    if final:
        hn = _rms(hn) * fg_ref[...]
    out_ref[...] = hn


def _ml_proj_kernel(h_ref, mod_ref, g_ref, wup_ref, xm_ref, z_ref):
    a = _norm_mod(h_ref[...], g_ref[...], mod_ref[0, 0:1, :], mod_ref[0, 1:2, :]).astype(BF16)
    y = _dot(a, wup_ref[...])
    xm_ref[...] = y[:, :ML_INNER]
    z_ref[...] = y[:, ML_INNER:]


def _ml_fin_kernel(hs_ref, xc_ref, z_ref, h_ref, mod_ref, gn_ref, skip_ref, wd_ref, out_ref):
    hs = hs_ref[0] + hs_ref[1]
    parts = []
    for i in range(ML_HEADS):
        o = hs[:, i * ML_HD:(i + 1) * ML_HD]
        parts.append(_rms(o - jnp.mean(o, axis=-1, keepdims=True)))
    hn = jnp.concatenate(parts, axis=-1) * gn_ref[...]
    y = ((hn + skip_ref[...] * xc_ref[...]) * _silu(z_ref[...])).astype(BF16)
    out_ref[...] = h_ref[...] + mod_ref[0, 2:3, :] * _dot(y, wd_ref[...])


def _norm_mod_kernel(h_ref, mod_ref, g_ref, a_ref):
    a_ref[...] = _norm_mod(h_ref[...], g_ref[...], mod_ref[0, 0:1, :], mod_ref[0, 1:2, :])


def _row_call(kernel, n_rows, tiles_per_batch, ctx_row, ins, in_specs, outs, name):
    return pl.pallas_call(
        kernel,
        grid=(n_rows // ROW_TILE,),
        in_specs=in_specs,
        out_specs=[_row_spec(w) for w, _ in outs],
        out_shape=[jax.ShapeDtypeStruct((n_rows, w), dt) for w, dt in outs],
        compiler_params=_params("arbitrary"),
        name=name,
    )(*ins)


def _scan_block(n_blocks):
    def block(d, s):
        return jnp.where(d == 0, s, jnp.where(s == 0, 0, n_blocks - s))
    return block


def _tri_pair(length, chunk, dtype):
    t = np.arange(length)
    same = (t[:, None] // chunk) == (t[None, :] // chunk)
    lower = (t[None, :] <= t[:, None]) & same
    upper = (t[None, :] >= t[:, None]) & same
    return jnp.asarray(np.stack([lower, upper]).astype(np.float32), dtype=dtype)


def _split3(x):
    x1 = x.astype(BF16)
    r1 = x - x1.astype(F32)
    x2 = r1.astype(BF16)
    x3 = (r1 - x2.astype(F32)).astype(BF16)
    return x1, x2, x3


def _gla_scan_kernel(q_ref, k_ref, v_ref, la_ref, tri_ref, o_ref, st_ref):
    fwd = pl.program_id(1) == 0
    n_chunks = ROW_TILE // CHUNK
    assert n_chunks == 4

    @pl.when(pl.program_id(2) == 0)
    def _():
        st_ref[...] = jnp.zeros_like(st_ref)

    rows = [slice(c * CHUNK, (c + 1) * CHUNK) for c in range(n_chunks)]
    g = la_ref[0]
    tri = tri_ref[0]
    cum = sum(_dot(tri, piece) for piece in _split3(g))
    tot = [jnp.sum(g[r], axis=0, keepdims=True) for r in rows]
    dec = [jnp.exp(t) for t in tot]
    tot_rows = jnp.concatenate([jnp.broadcast_to(t, (CHUNK, t.shape[1])) for t in tot], axis=0)
    q = q_ref[0]
    k = k_ref[0]
    q_dec = q * jnp.exp(cum)
    k_inv = (k * jnp.exp(-cum)).astype(BF16)
    k_rem = k * jnp.exp(tot_rows - cum)

    d12 = dec[1] * dec[2]
    before = [None, dec[0], dec[0] * dec[1], dec[0] * d12]
    after = [d12 * dec[3], dec[2] * dec[3], dec[3], None]
    one = jnp.ones_like(dec[0])
    head = [jnp.where(fwd, one if b is None else b, one if a is None else a)
            for b, a in zip(before, after)]
    tail = [jnp.where(fwd, one if a is None else a, one if b is None else b)
            for b, a in zip(before, after)]
    q_in = jnp.concatenate([q_dec[r] * head[c] for c, r in enumerate(rows)], axis=0).astype(BF16)
    k_out = jnp.concatenate([k_rem[r] * tail[c] for c, r in enumerate(rows)], axis=0).astype(BF16)
    dec_all = d12 * dec[0] * dec[3]

    between = {(0, 2): dec[1], (1, 3): dec[2], (0, 3): d12}
    def keys_for(c):
        parts = []
        for c2, r in enumerate(rows):
            if c2 == c:
                parts.append(k_inv[r])
            else:
                mid = between.get((min(c, c2), max(c, c2)))
                parts.append((k_rem[r] if mid is None else k_rem[r] * mid).astype(BF16))
        return jnp.concatenate(parts, axis=0)
    keys = [keys_for(c) for c in range(n_chunks)]
    q_dec = q_dec.astype(BF16)

    pos = lax.broadcasted_iota(jnp.int32, (ROW_TILE, ROW_TILE), 0)
    src = lax.broadcasted_iota(jnp.int32, (ROW_TILE, ROW_TILE), 1)
    keep = (pos - src) * jnp.where(fwd, 1, -1) >= 0

    for i in range(GLA_HEADS):
        kc = slice(i * GLA_HK, (i + 1) * GLA_HK)
        vc = slice(i * GLA_HV, (i + 1) * GLA_HV)
        v = v_ref[0, :, vc]
        att = jnp.concatenate(
            [lax.dot_general(q_dec[r, kc], keys[c][:, kc], NT_DIMS, preferred_element_type=F32)
             for c, r in enumerate(rows)], axis=0)
        att = jnp.where(keep, att, 0.0).astype(BF16)
        st = st_ref[i]
        o_ref[0, 0, :, vc] = _dot(att, v) + lax.dot_general(
            q_in[:, kc], st.astype(BF16), NT_DIMS, preferred_element_type=F32)
        st_ref[i] = st * dec_all[:, kc] + lax.dot_general(
            v, k_out[:, kc], TN_DIMS, preferred_element_type=F32)


def _gla_scan(q, k, v, la, batch, tokens):
    n_blocks = tokens // ROW_TILE
    block = _scan_block(n_blocks)
    q3, k3 = q.reshape(batch, tokens, GLA_DK), k.reshape(batch, tokens, GLA_DK)
    v3 = v.reshape(batch, tokens, GLA_DV)
    la3 = la.reshape(batch, tokens, 2 * GLA_DK)
    o = pl.pallas_call(
        _gla_scan_kernel,
        grid=(batch, 2, n_blocks),
        in_specs=[
            pl.BlockSpec((1, ROW_TILE, GLA_DK), lambda b, d, s: (b, block(d, s), 0)),
            pl.BlockSpec((1, ROW_TILE, GLA_DK), lambda b, d, s: (b, block(d, s), 0)),
            pl.BlockSpec((1, ROW_TILE, GLA_DV), lambda b, d, s: (b, block(d, s), 0)),
            pl.BlockSpec((1, ROW_TILE, GLA_DK), lambda b, d, s: (b, block(d, s), d)),
            pl.BlockSpec((1, ROW_TILE, ROW_TILE), lambda b, d, s: (d, 0, 0)),
        ],
        out_specs=pl.BlockSpec((1, 1, ROW_TILE, GLA_DV), lambda b, d, s: (d, b, block(d, s), 0)),
        out_shape=jax.ShapeDtypeStruct((2, batch, tokens, GLA_DV), F32),
        scratch_shapes=[pltpu.VMEM((GLA_HEADS, GLA_HV, GLA_HK), F32)],
        compiler_params=_params("arbitrary", "arbitrary", "arbitrary"),
        name="gla_scan",
    )(q3, k3, v3, la3, _tri_pair(ROW_TILE, CHUNK, BF16))
    return o.reshape(2, batch * tokens, GLA_DV)


def _ml_scan_kernel(q_ref, k_ref, v_ref, g_ref, tri_ref, o_ref, c_ref, n_ref, m_ref):
    H = ML_HEADS

    @pl.when(pl.program_id(2) == 0)
    def _():
        c_ref[...] = jnp.zeros_like(c_ref)
        n_ref[...] = jnp.zeros_like(n_ref)
        m_ref[...] = jnp.zeros_like(m_ref)

    tri = tri_ref[0]
    keep = tri > 0.0
    g = g_ref[0]
    cum = sum(_dot(tri, piece) for piece in _split3(g))
    tot = jnp.sum(g, axis=0, keepdims=True)
    g_t = g.T
    cum_t = cum.T
    for i in range(H):
        hc = slice(i * ML_HD, (i + 1) * ML_HD)
        i_col, b_col = g[:, i:i + 1], cum[:, H + i:H + i + 1]
        i_row, b_row = g_t[i:i + 1, :], cum_t[H + i:H + i + 1, :]
        b_last = tot[:, H + i:H + i + 1]
        q = q_ref[0, :, hc]
        k = k_ref[0, :, hc]
        v = v_ref[0, :, hc]
        m_prev = m_ref[i, 0:1, 0:1]
        d_mat = jnp.where(keep, b_col - b_row + i_row, -jnp.inf)
        inter = b_col + m_prev
        m_row = jnp.maximum(jnp.max(d_mat, axis=-1, keepdims=True), inter)
        qk = lax.dot_general(q, k, NT_DIMS, preferred_element_type=F32)
        w_qk = qk * jnp.exp(d_mat - m_row)
        w_inter = jnp.exp(inter - m_row)
        c_bar = c_ref[i]
        n_bar = n_ref[i]
        num = _dot(w_qk.astype(BF16), v) + w_inter * _dot(q, c_bar.astype(BF16))
        den = (jnp.sum(w_qk, axis=-1, keepdims=True)
               + w_inter * jnp.sum(q.astype(F32) * n_bar, axis=-1, keepdims=True))
        o_ref[0, 0, :, hc] = num / jnp.maximum(jnp.abs(den), jnp.exp(-m_row))
        carry_log = b_last + m_prev
        m_new = jnp.maximum(carry_log, jnp.max(b_last - b_row + i_row, axis=-1, keepdims=True))
        w_tok = jnp.exp(b_last - b_col + i_col - m_new)
        w_carry = jnp.exp(carry_log - m_new)
        k_w = k.astype(F32) * w_tok
        c_ref[i] = w_carry * c_bar + lax.dot_general(k_w.astype(BF16), v, TN_DIMS,
                                                     preferred_element_type=F32)
        n_ref[i] = w_carry * n_bar + jnp.sum(k_w, axis=0, keepdims=True)
        m_ref[i] = jnp.broadcast_to(m_new, m_ref.shape[1:])


def _ml_scan(q, k, v, gates, batch, tokens):
    n_blocks = tokens // ROW_TILE
    block = _scan_block(n_blocks)
    shape3 = (batch, tokens, ML_INNER)
    o = pl.pallas_call(
        _ml_scan_kernel,
        grid=(batch, 2, n_blocks),
        in_specs=[
            pl.BlockSpec((1, ROW_TILE, ML_INNER), lambda b, d, s: (b, block(d, s), 0)),
            pl.BlockSpec((1, ROW_TILE, ML_INNER), lambda b, d, s: (b, block(d, s), 0)),
            pl.BlockSpec((1, ROW_TILE, ML_INNER), lambda b, d, s: (b, block(d, s), 0)),
            pl.BlockSpec((1, ROW_TILE, 128), lambda b, d, s: (b, block(d, s), d)),
            pl.BlockSpec((1, ROW_TILE, ROW_TILE), lambda b, d, s: (d, 0, 0)),
        ],
        out_specs=pl.BlockSpec((1, 1, ROW_TILE, ML_INNER), lambda b, d, s: (d, b, block(d, s), 0)),
        out_shape=jax.ShapeDtypeStruct((2, batch, tokens, ML_INNER), F32),
        scratch_shapes=[pltpu.VMEM((ML_HEADS, ML_HD, ML_HD), F32),
                        pltpu.VMEM((ML_HEADS, 1, ML_HD), F32),
                        pltpu.VMEM((ML_HEADS, 8, 128), F32)],
        compiler_params=_params("arbitrary", "arbitrary", "arbitrary"),
        name="mlstm_scan",
    )(q.reshape(shape3), k.reshape(shape3), v.reshape(shape3),
      gates.reshape(batch, tokens, 256), _tri_pair(ROW_TILE, ROW_TILE, BF16))
    return o.reshape(2, batch * tokens, ML_INNER)


def _ml_conv_kernel(x_ref, xp_ref, xn_ref, cw_ref, cb_ref, wbd_ref, wg_ref, bg_ref,
                    xc_ref, q_ref, k_ref, v_ref, g_ref, ext_ref, *, n_blocks):
    t = pl.program_id(1)
    prev_ok = t >= 2
    next_ok = jnp.logical_and(t >= 1, t < n_blocks - 1)
    ext_ref[0:HALO, :] = jnp.where(prev_ok, xp_ref[0], 0.0)
    ext_ref[HALO:HALO + ROW_TILE, :] = x_ref[0]
    ext_ref[HALO + ROW_TILE:, :] = jnp.where(next_ok, xn_ref[0], 0.0)
    gates = jnp.zeros((ROW_TILE, 256), F32)
    for cb in range(ML_INNER // ML_BD):
        cols = slice(cb * ML_BD, (cb + 1) * ML_BD)
        y = cb_ref[:, cols]
        for tap in range(ML_CONV):
            off = HALO + tap - ML_CONV // 2
            y = y + cw_ref[tap:tap + 1, cols] * ext_ref[off:off + ROW_TILE, cols]
        xc = _silu(y)
        xc_ref[0, :, cols] = xc
        xc_b = xc.astype(BF16)
        xm_b = ext_ref[HALO:HALO + ROW_TILE, cols].astype(BF16)
        q = _dot(xc_b, wbd_ref[0, cb])
        k = _dot(xc_b, wbd_ref[1, cb])
        v = _dot(xm_b, wbd_ref[2, cb])
        q_b, k_b, v_b = q.astype(BF16), k.astype(BF16), v.astype(BF16)
        q_ref[0, :, cols] = q_b
        k_ref[0, :, cols] = (k * (ML_HD ** -0.5)).astype(BF16)
        v_ref[0, :, cols] = v_b
        for j, part in enumerate((q_b, k_b, v_b)):
            gates = gates + _dot(part, wg_ref[j * ML_INNER + cb * ML_BD:j * ML_INNER + (cb + 1) * ML_BD, :])
    gates = gates + bg_ref[...]
    lane = lax.broadcasted_iota(jnp.int32, gates.shape, 1) % 128
    is_forget = jnp.logical_and(lane >= ML_HEADS, lane < 2 * ML_HEADS)
    g_ref[0] = jnp.where(is_forget, _log_sigmoid(gates), gates)


def _ml_conv(xm, conv_w, conv_b, wbd, wg, bg, batch, tokens):
    n_blocks = tokens // ROW_TILE
    per = ROW_TILE // HALO
    last = tokens // HALO - 1
    xm3 = xm.reshape(batch, tokens, ML_INNER)
    blk = lambda b, t: (b, t, 0)
    full = lambda shape: pl.BlockSpec(shape, lambda b, t: (0,) * len(shape))
    outs = pl.pallas_call(
        functools.partial(_ml_conv_kernel, n_blocks=n_blocks),
        grid=(batch, n_blocks),
        in_specs=[
            pl.BlockSpec((1, ROW_TILE, ML_INNER), blk),
            pl.BlockSpec((1, HALO, ML_INNER), lambda b, t: (b, jnp.maximum(t * per - 1, 0), 0)),
            pl.BlockSpec((1, HALO, ML_INNER), lambda b, t: (b, jnp.minimum((t + 1) * per, last), 0)),
            full(conv_w.shape), full(conv_b.shape), full(wbd.shape), full(wg.shape), full(bg.shape),
        ],
        out_specs=[pl.BlockSpec((1, ROW_TILE, ML_INNER), blk)] * 4
        + [pl.BlockSpec((1, ROW_TILE, 256), blk)],
        out_shape=[jax.ShapeDtypeStruct((batch, tokens, ML_INNER), F32)]
        + [jax.ShapeDtypeStruct((batch, tokens, ML_INNER), BF16)] * 3
        + [jax.ShapeDtypeStruct((batch, tokens, 256), F32)],
        scratch_shapes=[pltpu.VMEM((ROW_TILE + 2 * HALO, ML_INNER), F32)],
        compiler_params=_params("arbitrary", "arbitrary"),
        name="mlstm_conv",
    )(xm3, xm3, xm3, conv_w, conv_b, wbd, wg, bg)
    n = batch * tokens
    xc, q, k, v, g = outs
    return (xc.reshape(n, ML_INNER), q.reshape(n, ML_INNER), k.reshape(n, ML_INNER),
            v.reshape(n, ML_INNER), g.reshape(n, 256))


def _pool_kernel(a_ref, h_ref, modl_ref, modc_ref, band_ref, wp_ref, bp_ref, sc_ref,
                 o_ref, y_ref, *, n_rows):
    group = pl.program_id(1)
    ctx = ROW_TILE
    wp = wp_ref[0]
    bp = bp_ref[0]
    sc = sc_ref[...]
    gate_l = modl_ref[0, 2:3, :]
    gate_c = modc_ref[0, 2:3, :]

    def mix(delta, res, gate):
        return res + gate * ((_dot(delta.astype(BF16), wp) + bp) * sc)

    def body(win):
        lo, hi = win // 2, win - 1 - win // 2

        def count(pos, length):
            return (jnp.minimum(pos + hi, length - 1) - jnp.maximum(pos - lo, 0) + 1).astype(F32)

        x = a_ref[0, 0:ctx, :]
        tpos = lax.broadcasted_iota(jnp.int32, (ctx, POOL_GROUP), 0)
        pooled = _dot_f32(band_ref[0, 0], x) / count(tpos, ctx)
        o_ref[0, 0:ctx, :] = mix(pooled - x, h_ref[0, 0:ctx, :], gate_c)

        band = band_ref[0, 1]

        def width_sum(tb, carry):
            src = pl.ds(pl.multiple_of(ctx + tb * ROW_TILE, ROW_TILE), ROW_TILE)
            dst = pl.ds(pl.multiple_of(tb * ROW_TILE, ROW_TILE), ROW_TILE)
            y_ref[dst, :] = _dot_f32(band, a_ref[0, src, :])
            return carry

        lax.fori_loop(0, n_rows * GRID_W // ROW_TILE, width_sum, 0)

        cpos = lax.broadcasted_iota(jnp.int32, (GRID_W, POOL_GROUP), 0)
        cnt_w = count(cpos, GRID_W)

        def height_sum(r, carry):
            acc = jnp.zeros((GRID_W, POOL_GROUP), F32)
            for dlt in range(-lo, hi + 1):
                rr = r + dlt
                valid = jnp.logical_and(rr >= 0, rr < n_rows)
                rc = jnp.clip(rr, 0, n_rows - 1)
                yv = y_ref[pl.ds(pl.multiple_of(rc * GRID_W, GRID_W), GRID_W), :]
                acc = acc + jnp.where(valid, yv, 0.0)
            cnt_h = count(r, n_rows)
            rows = pl.ds(pl.multiple_of(ctx + r * GRID_W, GRID_W), GRID_W)
            pooled = acc / (cnt_w * cnt_h)
            o_ref[0, rows, :] = mix(pooled - a_ref[0, rows, :], h_ref[0, rows, :], gate_l)
            return carry

        lax.fori_loop(0, n_rows, height_sum, 0)

    for gi, win in enumerate(POOL_WINDOWS):
        pl.when(group == gi)(functools.partial(body, win))


def _pool_bands():
    bands = np.zeros((len(POOL_WINDOWS), 2, ROW_TILE, ROW_TILE), np.float32)
    t = np.arange(ROW_TILE)
    for gi, win in enumerate(POOL_WINDOWS):
        lo, hi = win // 2, win - 1 - win // 2
        near = (t[None, :] - t[:, None] >= -lo) & (t[None, :] - t[:, None] <= hi)
        same_row = (t[None, :] // GRID_W) == (t[:, None] // GRID_W)
        bands[gi, 0] = near
        bands[gi, 1] = near & same_row
    return jnp.asarray(bands)


def _pool_mix(a, h, mod, w_pool, b_pool, scale, batch, tokens, ctx_row):
    n_groups = len(POOL_WINDOWS)
    lat = tokens - ROW_TILE
    a3, h3 = a.reshape(batch, tokens, D_MODEL), h.reshape(batch, tokens, D_MODEL)
    slab = pl.BlockSpec((1, tokens, POOL_GROUP), lambda b, g: (b, 0, g))
    out = pl.pallas_call(
        functools.partial(_pool_kernel, n_rows=lat // GRID_W),
        grid=(batch, n_groups),
        in_specs=[
            slab, slab,
            pl.BlockSpec((1, 6, POOL_GROUP), lambda b, g: (b, 0, g)),
            pl.BlockSpec((1, 6, POOL_GROUP), lambda b, g: (ctx_row, 0, g)),
            pl.BlockSpec((1, 2, ROW_TILE, ROW_TILE), lambda b, g: (g, 0, 0, 0)),
            pl.BlockSpec((1, POOL_GROUP, POOL_GROUP), lambda b, g: (g, 0, 0)),
            pl.BlockSpec((1, 1, POOL_GROUP), lambda b, g: (g, 0, 0)),
            pl.BlockSpec((1, POOL_GROUP), lambda b, g: (0, g)),
        ],
        out_specs=slab,
        out_shape=jax.ShapeDtypeStruct((batch, tokens, D_MODEL), F32),
        scratch_shapes=[pltpu.VMEM((lat, POOL_GROUP), F32)],
        compiler_params=_params("arbitrary", "arbitrary"),
        name="pool_mix",
    )(a3, h3, mod, mod, _pool_bands(), w_pool.astype(BF16),
      b_pool.reshape(n_groups, 1, POOL_GROUP), scale.reshape(1, D_MODEL))
    return out.reshape(batch * tokens, D_MODEL)


def _block_diag_tiles(w):
    per = ML_BD // ML_BLOCK
    w = w.reshape(-1, per, ML_BLOCK, ML_BLOCK)
    eye = jnp.eye(per, dtype=w.dtype)
    t = w[:, :, :, None, :] * eye[None, :, None, :, None]
    return t.reshape(-1, ML_BD, ML_BD)


def _gla_gate_weights(w_a1, w_a2, b_a):
    wa1 = jnp.zeros((D_MODEL, 128), F32)
    wa2 = jnp.zeros((128, 2 * GLA_DK), F32)
    for d in range(2):
        wa1 = wa1.at[:, d * GLA_RANK:(d + 1) * GLA_RANK].set(w_a1[d])
        wa2 = wa2.at[d * GLA_RANK:(d + 1) * GLA_RANK, d * GLA_DK:(d + 1) * GLA_DK].set(w_a2[d])
    return wa1.astype(BF16), wa2.astype(BF16), b_a.reshape(1, 2 * GLA_DK)


def _ml_gate_weights(w_gate, b_gate):
    wg = jnp.zeros((3 * ML_INNER, 256), F32)
    bg = jnp.zeros((1, 256), F32)
    for d in range(2):
        wg = wg.at[:, d * 128:d * 128 + 2 * ML_HEADS].set(w_gate[d])
        bg = bg.at[0, d * 128:d * 128 + 2 * ML_HEADS].set(b_gate[d])
    return wg.astype(BF16), bg


def kernel(x, c, ctx, c_ctx, ada_w, ada_b, norm1_g, norm2_g, mlp_w1, mlp_w2, gla_w_in, gla_w_a1, gla_w_a2, gla_b_a, gla_g_head, gla_w_o, mlstm_w_up, mlstm_conv_w, mlstm_conv_b, mlstm_w_qkv, mlstm_w_gate, mlstm_b_gate, mlstm_g_norm, mlstm_skip, mlstm_w_down, pool_w, pool_b, pool_scale, final_g):
    batch, seq, d = x.shape
    ctx_len = ctx.shape[1]
    depth = ada_w.shape[0]
    assert d == D_MODEL and ctx_len == ROW_TILE and seq % ROW_TILE == 0 and batch < MOD_ROWS
    tokens = ctx_len + seq
    n = batch * tokens
    tpb = tokens // ROW_TILE
    ctx_row = batch

    cc = jnp.zeros((MOD_ROWS, d), F32).at[:batch].set(c).at[ctx_row].set(c_ctx)
    mod_all = _mod_table(cc, ada_w, ada_b).reshape(depth, MOD_ROWS, 6, d)

    h = jnp.concatenate([ctx, x], axis=1).reshape(n, d)
    mod_spec = _mod_spec(tpb, ctx_row)
    vec = lambda w: _full_spec((1, w))
    row_call = functools.partial(_row_call, n_rows=n, tiles_per_batch=tpb, ctx_row=ctx_row)

    for i in range(depth):
        kind, j = i % N_MIXERS, i // N_MIXERS
        mod = mod_all[i]
        g1 = norm1_g[i].reshape(1, d)
        if kind == 0:
            w_in = gla_w_in[j].astype(BF16)
            wa1, wa2, ba = _gla_gate_weights(gla_w_a1[j], gla_w_a2[j], gla_b_a[j])
            q, k, v, r, la = row_call(
                _gla_proj_kernel,
                ins=(h, mod, g1, w_in, wa1, wa2, ba),
                in_specs=[_row_spec(d), mod_spec, vec(d), _full_spec(w_in.shape),
                          _full_spec(wa1.shape), _full_spec(wa2.shape), vec(2 * GLA_DK)],
                outs=[(GLA_DK, F32), (GLA_DK, F32), (GLA_DV, BF16), (GLA_DV, F32), (2 * GLA_DK, F32)],
                name="gla_proj")
            o = _gla_scan(q, k, v, la, batch, tokens)
            w_o = gla_w_o[j].astype(BF16)
            (h,) = row_call(
                _gla_fin_kernel,
                ins=(o, r, h, mod, gla_g_head[j].reshape(1, GLA_DV), w_o),
                in_specs=[pl.BlockSpec((2, ROW_TILE, GLA_DV), lambda t: (0, t, 0)), _row_spec(GLA_DV),
                          _row_spec(d), mod_spec, vec(GLA_DV), _full_spec(w_o.shape)],
                outs=[(d, F32)], name="gla_fin")
        elif kind == 1:
            w_up = mlstm_w_up[j].astype(BF16)
            xm, z = row_call(
                _ml_proj_kernel,
                ins=(h, mod, g1, w_up),
                in_specs=[_row_spec(d), mod_spec, vec(d), _full_spec(w_up.shape)],
                outs=[(ML_INNER, F32), (ML_INNER, F32)], name="mlstm_proj")
            wbd = jnp.stack([_block_diag_tiles(mlstm_w_qkv[j, p]) for p in range(3)]).astype(BF16)
            wg, bg = _ml_gate_weights(mlstm_w_gate[j], mlstm_b_gate[j])
            xc, q, k, v, gates = _ml_conv(xm, mlstm_conv_w[j], mlstm_conv_b[j].reshape(1, ML_INNER),
                                          wbd, wg, bg, batch, tokens)
            hs = _ml_scan(q, k, v, gates, batch, tokens)
            w_down = mlstm_w_down[j].astype(BF16)
            (h,) = row_call(
                _ml_fin_kernel,
                ins=(hs, xc, z, h, mod, mlstm_g_norm[j].reshape(1, ML_INNER),
                     mlstm_skip[j].reshape(1, ML_INNER), w_down),
                in_specs=[pl.BlockSpec((2, ROW_TILE, ML_INNER), lambda t: (0, t, 0)),
                          _row_spec(ML_INNER), _row_spec(ML_INNER), _row_spec(d), mod_spec,
                          vec(ML_INNER), vec(ML_INNER), _full_spec(w_down.shape)],
                outs=[(d, F32)], name="mlstm_fin")
        else:
            (a,) = row_call(
                _norm_mod_kernel, ins=(h, mod, g1),
                in_specs=[_row_spec(d), mod_spec, vec(d)], outs=[(d, F32)], name="pool_norm")
            h = _pool_mix(a, h, mod, pool_w[j], pool_b[j], pool_scale[j], batch, tokens, ctx_row)

        w1, w2 = mlp_w1[i].astype(BF16), mlp_w2[i].astype(BF16)
        final = i == depth - 1
        (h,) = row_call(
            functools.partial(_mlp_kernel, final=final),
            ins=(h, mod, norm2_g[i].reshape(1, d), w1, w2, final_g.reshape(1, d)),
            in_specs=[_row_spec(d), mod_spec, vec(d), _full_spec(w1.shape), _full_spec(w2.shape), vec(d)],
            outs=[(d, F32)], name="mlp")

    return h.reshape(batch, tokens, d)[:, ctx_len:, :]
```

```python
import functools

import numpy as np
import jax
import jax.numpy as jnp
from jax import lax
from jax.experimental import pallas as pl
from jax.experimental.pallas import tpu as pltpu

F32 = jnp.float32
BF16 = jnp.bfloat16
HIGHEST = lax.Precision.HIGHEST

D_MODEL = 1024
D_FF = 4 * D_MODEL
DEPTH = 4
N_MIXERS = 3
EPS = 1e-6
GRID_W = 64
ROW_TILE = 256
CHUNK = 64
MOD_ROWS = 16

GLA_HEADS = 4
GLA_DK = D_MODEL // 2
GLA_DV = D_MODEL
GLA_HK = GLA_DK // GLA_HEADS
GLA_HV = GLA_DV // GLA_HEADS
GLA_IN = 2 * GLA_DK + 2 * GLA_DV
GLA_RANK = 16
GLA_TAU = 16.0

ML_HEADS = 4
ML_INNER = 2 * D_MODEL
ML_HD = ML_INNER // ML_HEADS
ML_BLOCK = 4
ML_CONV = 4
ML_BD = 256
HALO = 8

POOL_WINDOWS = (2, 4, 8, 16)
POOL_GROUP = D_MODEL // len(POOL_WINDOWS)

VMEM_LIMIT = 56 * 1024 * 1024

NT_DIMS = (((1,), (1,)), ((), ()))
TN_DIMS = (((0,), (0,)), ((), ()))


def _params(*sem):
    return pltpu.CompilerParams(dimension_semantics=sem, vmem_limit_bytes=VMEM_LIMIT)


def _silu(x):
    return x * jax.nn.sigmoid(x)


def _log_sigmoid(z):
    return jnp.minimum(z, 0.0) - jnp.log1p(jnp.exp(-jnp.abs(z)))


def _rms(x):
    return x * lax.rsqrt(jnp.mean(x * x, axis=-1, keepdims=True) + EPS)


def _norm_mod(h, g, shift, scale):
    return (_rms(h) * g) * (1.0 + scale) + shift


def _dot(a, b):
    return jnp.dot(a, b, preferred_element_type=F32)


def _dot_f32(a, b):
    return jnp.dot(a, b, precision=HIGHEST, preferred_element_type=F32)


def _split3(x):
    x1 = x.astype(BF16)
    r1 = x - x1.astype(F32)
    x2 = r1.astype(BF16)
    x3 = (r1 - x2.astype(F32)).astype(BF16)
    return x1, x2, x3


def _dot_01(m, x):
    return sum(_dot(m, piece) for piece in _split3(x))


def _mod_kernel(cc_ref, w_ref, b_ref, o_ref):
    s = _silu(cc_ref[...])
    o_ref[0] = _dot_f32(s, w_ref[0]) + b_ref[0]


def _mod_table(cc, ada_w, ada_b):
    depth, d, n = ada_w.shape
    tn = 1536
    return pl.pallas_call(
        _mod_kernel,
        grid=(depth, n // tn),
        in_specs=[
            pl.BlockSpec((MOD_ROWS, d), lambda i, j: (0, 0)),
            pl.BlockSpec((1, d, tn), lambda i, j: (i, 0, j)),
            pl.BlockSpec((1, 1, tn), lambda i, j: (i, 0, j)),
        ],
        out_specs=pl.BlockSpec((1, MOD_ROWS, tn), lambda i, j: (i, 0, j)),
        out_shape=jax.ShapeDtypeStruct((depth, MOD_ROWS, n), F32),
        compiler_params=_params("arbitrary", "arbitrary"),
        name="mod_table",
    )(cc, ada_w, ada_b.reshape(depth, 1, n))


def _identity(t):
    return t


def _row_spec(width, src=_identity):
    return pl.BlockSpec((ROW_TILE, width), lambda t: (src(t), 0))


def _pair_spec(width, src=_identity):
    return pl.BlockSpec((2, ROW_TILE, width), lambda t: (0, src(t), 0))


def _full_spec(shape):
    nd = len(shape)
    return pl.BlockSpec(shape, lambda t: (0,) * nd)


def _mod_spec(mod_row):
    return pl.BlockSpec((1, 6, D_MODEL), lambda t: (mod_row(t), 0, 0))


def _gla_proj_kernel(h_ref, mod_ref, g_ref, win_ref, wa1_ref, wa2_ref, ba_ref,
                     q_ref, k_ref, v_ref, r_ref, la_ref):
    a = _norm_mod(h_ref[...], g_ref[...], mod_ref[0, 0:1, :], mod_ref[0, 1:2, :]).astype(BF16)
    y = _dot(a, win_ref[...])
    q_ref[...] = y[:, :GLA_DK] * (GLA_HK ** -0.5)
    k_ref[...] = y[:, GLA_DK:2 * GLA_DK]
    v_ref[...] = y[:, 2 * GLA_DK:2 * GLA_DK + GLA_DV].astype(BF16)
    r_ref[...] = y[:, 2 * GLA_DK + GLA_DV:].astype(r_ref.dtype)
    t = _dot(a, wa1_ref[...]).astype(BF16)
    z = _dot(t, wa2_ref[...]) + ba_ref[...]
    la_ref[...] = _log_sigmoid(z) * (1.0 / GLA_TAU)


def _gla_fin_kernel(o_ref, r_ref, h_ref, mod_ref, gh_ref, wo_ref, out_ref):
    o = o_ref[0].astype(F32) + o_ref[1].astype(F32)
    parts = [_rms(o[:, i * GLA_HV:(i + 1) * GLA_HV]) for i in range(GLA_HEADS)]
    on = jnp.concatenate(parts, axis=-1) * gh_ref[...]
    y = (on * _silu(r_ref[...].astype(F32))).astype(BF16)
    out_ref[...] = h_ref[...] + mod_ref[0, 2:3, :] * _dot(y, wo_ref[...])


def _mlp_kernel(h_ref, mod_ref, g_ref, w1_ref, w2_ref, fg_ref, out_ref, *, final):
    h = h_ref[...]
    a = _norm_mod(h, g_ref[...], mod_ref[0, 3:4, :], mod_ref[0, 4:5, :]).astype(BF16)
    u = jnp.square(jnp.maximum(_dot(a, w1_ref[...]), 0.0)).astype(BF16)
    hn = h + mod_ref[0, 5:6, :] * _dot(u, w2_ref[...])
    if final:
        hn = _rms(hn) * fg_ref[...]
    out_ref[...] = hn


def _ml_proj_kernel(h_ref, mod_ref, g_ref, wup_ref, xm_ref, z_ref):
    a = _norm_mod(h_ref[...], g_ref[...], mod_ref[0, 0:1, :], mod_ref[0, 1:2, :]).astype(BF16)
    y = _dot(a, wup_ref[...])
    xm_ref[...] = y[:, :ML_INNER]
    z_ref[...] = y[:, ML_INNER:].astype(z_ref.dtype)


def _ml_fin_kernel(hs_ref, xc_ref, z_ref, h_ref, mod_ref, gn_ref, skip_ref, wd_ref, out_ref):
    hs = hs_ref[0].astype(F32) + hs_ref[1].astype(F32)
    parts = []
    for i in range(ML_HEADS):
        o = hs[:, i * ML_HD:(i + 1) * ML_HD]
        parts.append(_rms(o - jnp.mean(o, axis=-1, keepdims=True)))
    hn = jnp.concatenate(parts, axis=-1) * gn_ref[...]
    y = ((hn + skip_ref[...] * xc_ref[...].astype(F32)) * _silu(z_ref[...].astype(F32))).astype(BF16)
    out_ref[...] = h_ref[...] + mod_ref[0, 2:3, :] * _dot(y, wd_ref[...])


def _norm_mod_kernel(h_ref, mod_ref, g_ref, a_ref):
    a_ref[...] = _norm_mod(h_ref[...], g_ref[...], mod_ref[0, 0:1, :], mod_ref[0, 1:2, :])


def _row_call(kernel, n_tiles, ins, in_specs, outs, name):
    return pl.pallas_call(
        kernel,
        grid=(n_tiles,),
        in_specs=in_specs,
        out_specs=[_row_spec(w) for w, _ in outs],
        out_shape=[jax.ShapeDtypeStruct((n_tiles * ROW_TILE, w), dt) for w, dt in outs],
        compiler_params=_params("arbitrary"),
        name=name,
    )(*ins)


def _scan_block(n_blocks):
    def block(d, s):
        return jnp.where(d == 0, s, jnp.where(s == 0, 0, n_blocks - s))
    return block


def _tri_pair(length, chunk, dtype):
    t = np.arange(length)
    same = (t[:, None] // chunk) == (t[None, :] // chunk)
    lower = (t[None, :] <= t[:, None]) & same
    upper = (t[None, :] >= t[:, None]) & same
    return jnp.asarray(np.stack([lower, upper]).astype(np.float32), dtype=dtype)


def _gla_scan_kernel(q_ref, k_ref, v_ref, la_ref, tri_ref, o_ref, st_ref):
    fwd = pl.program_id(1) == 0
    n_chunks = ROW_TILE // CHUNK
    assert n_chunks == 4

    @pl.when(pl.program_id(2) == 0)
    def _():
        st_ref[...] = jnp.zeros_like(st_ref)

    rows = [slice(c * CHUNK, (c + 1) * CHUNK) for c in range(n_chunks)]
    g = la_ref[0]
    tri = tri_ref[0]
    cum = _dot_01(tri, g)
    tot = [jnp.sum(g[r], axis=0, keepdims=True) for r in rows]
    dec = [jnp.exp(t) for t in tot]
    tot_rows = jnp.concatenate([jnp.broadcast_to(t, (CHUNK, t.shape[1])) for t in tot], axis=0)
    q = q_ref[0]
    k = k_ref[0]
    q_dec = q * jnp.exp(cum)
    k_inv = (k * jnp.exp(-cum)).astype(BF16)
    k_rem = k * jnp.exp(tot_rows - cum)

    d12 = dec[1] * dec[2]
    before = [None, dec[0], dec[0] * dec[1], dec[0] * d12]
    after = [d12 * dec[3], dec[2] * dec[3], dec[3], None]
    one = jnp.ones_like(dec[0])
    head = [jnp.where(fwd, one if b is None else b, one if a is None else a)
            for b, a in zip(before, after)]
    tail = [jnp.where(fwd, one if a is None else a, one if b is None else b)
            for b, a in zip(before, after)]
    q_in = jnp.concatenate([q_dec[r] * head[c] for c, r in enumerate(rows)], axis=0).astype(BF16)
    k_out = jnp.concatenate([k_rem[r] * tail[c] for c, r in enumerate(rows)], axis=0).astype(BF16)
    dec_all = d12 * dec[0] * dec[3]

    between = {(0, 2): dec[1], (1, 3): dec[2], (0, 3): d12}
    def keys_for(c):
        parts = []
        for c2, r in enumerate(rows):
            if c2 == c:
                parts.append(k_inv[r])
            else:
                mid = between.get((min(c, c2), max(c, c2)))
                parts.append((k_rem[r] if mid is None else k_rem[r] * mid).astype(BF16))
        return jnp.concatenate(parts, axis=0)
    keys = [keys_for(c) for c in range(n_chunks)]
    q_dec = q_dec.astype(BF16)

    pos = lax.broadcasted_iota(jnp.int32, (ROW_TILE, ROW_TILE), 0)
    src = lax.broadcasted_iota(jnp.int32, (ROW_TILE, ROW_TILE), 1)
    keep = (pos - src) * jnp.where(fwd, 1, -1) >= 0

    for i in range(GLA_HEADS):
        kc = slice(i * GLA_HK, (i + 1) * GLA_HK)
        vc = slice(i * GLA_HV, (i + 1) * GLA_HV)
        v = v_ref[0, :, vc]
        att = jnp.concatenate(
            [lax.dot_general(q_dec[r, kc], keys[c][:, kc], NT_DIMS, preferred_element_type=F32)
             for c, r in enumerate(rows)], axis=0)
        att = jnp.where(keep, att, 0.0).astype(BF16)
        st = st_ref[i]
        o_ref[0, 0, :, vc] = (_dot(att, v) + lax.dot_general(
            q_in[:, kc], st.astype(BF16), NT_DIMS, preferred_element_type=F32)).astype(o_ref.dtype)
        st_ref[i] = st * dec_all[:, kc] + lax.dot_general(
            v, k_out[:, kc], TN_DIMS, preferred_element_type=F32)


def _gla_scan(q, k, v, la, batch, tokens):
    n_blocks = tokens // ROW_TILE
    block = _scan_block(n_blocks)
    q3, k3 = q.reshape(batch, tokens, GLA_DK), k.reshape(batch, tokens, GLA_DK)
    v3 = v.reshape(batch, tokens, GLA_DV)
    la3 = la.reshape(batch, tokens, 2 * GLA_DK)
    o = pl.pallas_call(
        _gla_scan_kernel,
        grid=(batch, 2, n_blocks),
        in_specs=[
            pl.BlockSpec((1, ROW_TILE, GLA_DK), lambda b, d, s: (b, block(d, s), 0)),
            pl.BlockSpec((1, ROW_TILE, GLA_DK), lambda b, d, s: (b, block(d, s), 0)),
            pl.BlockSpec((1, ROW_TILE, GLA_DV), lambda b, d, s: (b, block(d, s), 0)),
            pl.BlockSpec((1, ROW_TILE, GLA_DK), lambda b, d, s: (b, block(d, s), d)),
            pl.BlockSpec((1, ROW_TILE, ROW_TILE), lambda b, d, s: (d, 0, 0)),
        ],
        out_specs=pl.BlockSpec((1, 1, ROW_TILE, GLA_DV), lambda b, d, s: (d, b, block(d, s), 0)),
        out_shape=jax.ShapeDtypeStruct((2, batch, tokens, GLA_DV), BF16),
        scratch_shapes=[pltpu.VMEM((GLA_HEADS, GLA_HV, GLA_HK), F32)],
        compiler_params=_params("arbitrary", "arbitrary", "arbitrary"),
        name="gla_scan",
    )(q3, k3, v3, la3, _tri_pair(ROW_TILE, CHUNK, BF16))
    return o.reshape(2, batch * tokens, GLA_DV)


def _ml_scan_kernel(q_ref, k_ref, v_ref, g_ref, tri_ref, o_ref, c_ref, n_ref, m_ref):
    H = ML_HEADS

    @pl.when(pl.program_id(2) == 0)
    def _():
        c_ref[...] = jnp.zeros_like(c_ref)
        n_ref[...] = jnp.zeros_like(n_ref)
        m_ref[...] = jnp.zeros_like(m_ref)

    tri = tri_ref[0]
    keep = tri > 0.0
    g = g_ref[0]
    cum = _dot_01(tri, g)
    tot = jnp.sum(g, axis=0, keepdims=True)
    g_t = g.T
    cum_t = cum.T
    for i in range(H):
        hc = slice(i * ML_HD, (i + 1) * ML_HD)
        i_col, b_col = g[:, i:i + 1], cum[:, H + i:H + i + 1]
        i_row, b_row = g_t[i:i + 1, :], cum_t[H + i:H + i + 1, :]
        b_last = tot[:, H + i:H + i + 1]
        q = q_ref[0, :, hc]
        k = k_ref[0, :, hc]
        v = v_ref[0, :, hc]
        m_prev = m_ref[i, 0:1, 0:1]
        d_mat = jnp.where(keep, b_col - b_row + i_row, -jnp.inf)
        inter = b_col + m_prev
        m_row = jnp.maximum(jnp.max(d_mat, axis=-1, keepdims=True), inter)
        qk = lax.dot_general(q, k, NT_DIMS, preferred_element_type=F32)
        w_qk = qk * jnp.exp(d_mat - m_row)
        w_inter = jnp.exp(inter - m_row)
        c_bar = c_ref[i]
        n_bar = n_ref[i]
        num = _dot(w_qk.astype(BF16), v) + w_inter * _dot(q, c_bar.astype(BF16))
        den = (jnp.sum(w_qk, axis=-1, keepdims=True)
               + w_inter * jnp.sum(q.astype(F32) * n_bar, axis=-1, keepdims=True))
        o_ref[0, 0, :, hc] = (num / jnp.maximum(jnp.abs(den), jnp.exp(-m_row))).astype(o_ref.dtype)
        carry_log = b_last + m_prev
        m_new = jnp.maximum(carry_log, jnp.max(b_last - b_row + i_row, axis=-1, keepdims=True))
        w_tok = jnp.exp(b_last - b_col + i_col - m_new)
        w_carry = jnp.exp(carry_log - m_new)
        k_w = k.astype(F32) * w_tok
        c_ref[i] = w_carry * c_bar + lax.dot_general(k_w.astype(BF16), v, TN_DIMS,
                                                     preferred_element_type=F32)
        n_ref[i] = w_carry * n_bar + jnp.sum(k_w, axis=0, keepdims=True)
        m_ref[i] = jnp.broadcast_to(m_new, m_ref.shape[1:])


def _ml_scan(q, k, v, gates, batch, tokens):
    n_blocks = tokens // ROW_TILE
    block = _scan_block(n_blocks)
    shape3 = (batch, tokens, ML_INNER)
    o = pl.pallas_call(
        _ml_scan_kernel,
        grid=(batch, 2, n_blocks),
        in_specs=[
            pl.BlockSpec((1, ROW_TILE, ML_INNER), lambda b, d, s: (b, block(d, s), 0)),
            pl.BlockSpec((1, ROW_TILE, ML_INNER), lambda b, d, s: (b, block(d, s), 0)),
            pl.BlockSpec((1, ROW_TILE, ML_INNER), lambda b, d, s: (b, block(d, s), 0)),
            pl.BlockSpec((1, ROW_TILE, 128), lambda b, d, s: (b, block(d, s), d)),
            pl.BlockSpec((1, ROW_TILE, ROW_TILE), lambda b, d, s: (d, 0, 0)),
        ],
        out_specs=pl.BlockSpec((1, 1, ROW_TILE, ML_INNER), lambda b, d, s: (d, b, block(d, s), 0)),
        out_shape=jax.ShapeDtypeStruct((2, batch, tokens, ML_INNER), BF16),
        scratch_shapes=[pltpu.VMEM((ML_HEADS, ML_HD, ML_HD), F32),
                        pltpu.VMEM((ML_HEADS, 1, ML_HD), F32),
                        pltpu.VMEM((ML_HEADS, 8, 128), F32)],
        compiler_params=_params("arbitrary", "arbitrary", "arbitrary"),
        name="mlstm_scan",
    )(q.reshape(shape3), k.reshape(shape3), v.reshape(shape3),
      gates.reshape(batch, tokens, 256), _tri_pair(ROW_TILE, ROW_TILE, BF16))
    return o.reshape(2, batch * tokens, ML_INNER)


def _ml_conv_kernel(x_ref, xp_ref, xn_ref, cw_ref, cb_ref, wbd_ref, wg_ref, bg_ref,
                    xc_ref, q_ref, k_ref, v_ref, g_ref, ext_ref, *, n_blocks):
    t = pl.program_id(1)
    prev_ok = t >= 2
    next_ok = jnp.logical_and(t >= 1, t < n_blocks - 1)
    ext_ref[0:HALO, :] = jnp.where(prev_ok, xp_ref[0], 0.0)
    ext_ref[HALO:HALO + ROW_TILE, :] = x_ref[0]
    ext_ref[HALO + ROW_TILE:, :] = jnp.where(next_ok, xn_ref[0], 0.0)
    gates = jnp.zeros((ROW_TILE, 256), F32)
    for cb in range(ML_INNER // ML_BD):
        cols = slice(cb * ML_BD, (cb + 1) * ML_BD)
        y = cb_ref[:, cols]
        for tap in range(ML_CONV):
            off = HALO + tap - ML_CONV // 2
            y = y + cw_ref[tap:tap + 1, cols] * ext_ref[off:off + ROW_TILE, cols]
        xc = _silu(y)
        xc_b = xc.astype(BF16)
        xc_ref[0, :, cols] = xc_b
        xm_b = ext_ref[HALO:HALO + ROW_TILE, cols].astype(BF16)
        q = _dot(xc_b, wbd_ref[0, cb])
        k = _dot(xc_b, wbd_ref[1, cb])
        v = _dot(xm_b, wbd_ref[2, cb])
        q_b, k_b, v_b = q.astype(BF16), k.astype(BF16), v.astype(BF16)
        q_ref[0, :, cols] = q_b
        k_ref[0, :, cols] = (k * (ML_HD ** -0.5)).astype(BF16)
        v_ref[0, :, cols] = v_b
        for j, part in enumerate((q_b, k_b, v_b)):
            gates = gates + _dot(part, wg_ref[j * ML_INNER + cb * ML_BD:j * ML_INNER + (cb + 1) * ML_BD, :])
    gates = gates + bg_ref[...]
    lane = lax.broadcasted_iota(jnp.int32, gates.shape, 1) % 128
    is_forget = jnp.logical_and(lane >= ML_HEADS, lane < 2 * ML_HEADS)
    g_ref[0] = jnp.where(is_forget, _log_sigmoid(gates), gates)


def _ml_conv(xm, conv_w, conv_b, wbd, wg, bg, batch, tokens):
    n_blocks = tokens // ROW_TILE
    per = ROW_TILE // HALO
    last = tokens // HALO - 1
    xm3 = xm.reshape(batch, tokens, ML_INNER)
    blk = lambda b, t: (b, t, 0)
    full = lambda shape: pl.BlockSpec(shape, lambda b, t: (0,) * len(shape))
    outs = pl.pallas_call(
        functools.partial(_ml_conv_kernel, n_blocks=n_blocks),
        grid=(batch, n_blocks),
        in_specs=[
            pl.BlockSpec((1, ROW_TILE, ML_INNER), blk),
            pl.BlockSpec((1, HALO, ML_INNER), lambda b, t: (b, jnp.maximum(t * per - 1, 0), 0)),
            pl.BlockSpec((1, HALO, ML_INNER), lambda b, t: (b, jnp.minimum((t + 1) * per, last), 0)),
            full(conv_w.shape), full(conv_b.shape), full(wbd.shape), full(wg.shape), full(bg.shape),
        ],
        out_specs=[pl.BlockSpec((1, ROW_TILE, ML_INNER), blk)] * 4
        + [pl.BlockSpec((1, ROW_TILE, 256), blk)],
        out_shape=[jax.ShapeDtypeStruct((batch, tokens, ML_INNER), BF16)] * 4
        + [jax.ShapeDtypeStruct((batch, tokens, 256), F32)],
        scratch_shapes=[pltpu.VMEM((ROW_TILE + 2 * HALO, ML_INNER), F32)],
        compiler_params=_params("arbitrary", "arbitrary"),
        name="mlstm_conv",
    )(xm3, xm3, xm3, conv_w, conv_b, wbd, wg, bg)
    n = batch * tokens
    xc, q, k, v, g = outs
    return (xc.reshape(n, ML_INNER), q.reshape(n, ML_INNER), k.reshape(n, ML_INNER),
            v.reshape(n, ML_INNER), g.reshape(n, 256))


def _pool_kernel(a_ref, h_ref, modl_ref, modc_ref, band_ref, wp_ref, bp_ref, sc_ref,
                 o_ref, y_ref, *, n_rows):
    group = pl.program_id(1)
    ctx = ROW_TILE
    wp = wp_ref[0]
    bp = bp_ref[0]
    sc = sc_ref[...]
    gate_l = modl_ref[0, 2:3, :]
    gate_c = modc_ref[0, 2:3, :]

    def mix(delta, res, gate):
        return res + gate * ((_dot(delta.astype(BF16), wp) + bp) * sc)

    def body(win):
        lo, hi = win // 2, win - 1 - win // 2

        def count(pos, length):
            return (jnp.minimum(pos + hi, length - 1) - jnp.maximum(pos - lo, 0) + 1).astype(F32)

        x = a_ref[0, 0:ctx, :]
        tpos = lax.broadcasted_iota(jnp.int32, (ctx, POOL_GROUP), 0)
        pooled = _dot_01(band_ref[0, 0], x) / count(tpos, ctx)
        o_ref[0, 0:ctx, :] = mix(pooled - x, h_ref[0, 0:ctx, :], gate_c)

        band = band_ref[0, 1]

        def width_sum(tb, carry):
            src = pl.ds(pl.multiple_of(ctx + tb * ROW_TILE, ROW_TILE), ROW_TILE)
            dst = pl.ds(pl.multiple_of(tb * ROW_TILE, ROW_TILE), ROW_TILE)
            y_ref[dst, :] = _dot_01(band, a_ref[0, src, :])
            return carry

        lax.fori_loop(0, n_rows * GRID_W // ROW_TILE, width_sum, 0)

        cpos = lax.broadcasted_iota(jnp.int32, (GRID_W, POOL_GROUP), 0)
        cnt_w = count(cpos, GRID_W)

        def height_sum(r, carry):
            acc = jnp.zeros((GRID_W, POOL_GROUP), F32)
            for dlt in range(-lo, hi + 1):
                rr = r + dlt
                valid = jnp.logical_and(rr >= 0, rr < n_rows)
                rc = jnp.clip(rr, 0, n_rows - 1)
                yv = y_ref[pl.ds(pl.multiple_of(rc * GRID_W, GRID_W), GRID_W), :]
                acc = acc + jnp.where(valid, yv, 0.0)
            cnt_h = count(r, n_rows)
            rows = pl.ds(pl.multiple_of(ctx + r * GRID_W, GRID_W), GRID_W)
            pooled = acc / (cnt_w * cnt_h)
            o_ref[0, rows, :] = mix(pooled - a_ref[0, rows, :], h_ref[0, rows, :], gate_l)
            return carry

        lax.fori_loop(0, n_rows, height_sum, 0)

    for gi, win in enumerate(POOL_WINDOWS):
        pl.when(group == gi)(functools.partial(body, win))


def _pool_bands():
    bands = np.zeros((len(POOL_WINDOWS), 2, ROW_TILE, ROW_TILE), np.float32)
    t = np.arange(ROW_TILE)
    for gi, win in enumerate(POOL_WINDOWS):
        lo, hi = win // 2, win - 1 - win // 2
        near = (t[None, :] - t[:, None] >= -lo) & (t[None, :] - t[:, None] <= hi)
        same_row = (t[None, :] // GRID_W) == (t[:, None] // GRID_W)
        bands[gi, 0] = near
        bands[gi, 1] = near & same_row
    return jnp.asarray(bands, dtype=BF16)


def _pool_mix(a, h, mod, w_pool, b_pool, scale, batch, tokens, ctx_row):
    n_groups = len(POOL_WINDOWS)
    lat = tokens - ROW_TILE
    a3, h3 = a.reshape(batch, tokens, D_MODEL), h.reshape(batch, tokens, D_MODEL)
    slab = pl.BlockSpec((1, tokens, POOL_GROUP), lambda b, g: (b, 0, g))
    out = pl.pallas_call(
        functools.partial(_pool_kernel, n_rows=lat // GRID_W),
        grid=(batch, n_groups),
        in_specs=[
            slab, slab,
            pl.BlockSpec((1, 6, POOL_GROUP), lambda b, g: (b, 0, g)),
            pl.BlockSpec((1, 6, POOL_GROUP), lambda b, g: (ctx_row, 0, g)),
            pl.BlockSpec((1, 2, ROW_TILE, ROW_TILE), lambda b, g: (g, 0, 0, 0)),
            pl.BlockSpec((1, POOL_GROUP, POOL_GROUP), lambda b, g: (g, 0, 0)),
            pl.BlockSpec((1, 1, POOL_GROUP), lambda b, g: (g, 0, 0)),
            pl.BlockSpec((1, POOL_GROUP), lambda b, g: (0, g)),
        ],
        out_specs=slab,
        out_shape=jax.ShapeDtypeStruct((batch, tokens, D_MODEL), F32),
        scratch_shapes=[pltpu.VMEM((lat, POOL_GROUP), F32)],
        compiler_params=_params("arbitrary", "arbitrary"),
        name="pool_mix",
    )(a3, h3, mod, mod, _pool_bands(), w_pool.astype(BF16),
      b_pool.reshape(n_groups, 1, POOL_GROUP), scale.reshape(1, D_MODEL))
    return out.reshape(batch * tokens, D_MODEL)


def _block_diag_tiles(w):
    per = ML_BD // ML_BLOCK
    w = w.reshape(-1, per, ML_BLOCK, ML_BLOCK)
    eye = jnp.eye(per, dtype=w.dtype)
    t = w[:, :, :, None, :] * eye[None, :, None, :, None]
    return t.reshape(-1, ML_BD, ML_BD)


def _gla_gate_weights(w_a1, w_a2, b_a):
    wa1 = jnp.zeros((D_MODEL, 128), F32)
    wa2 = jnp.zeros((128, 2 * GLA_DK), F32)
    for d in range(2):
        wa1 = wa1.at[:, d * GLA_RANK:(d + 1) * GLA_RANK].set(w_a1[d])
        wa2 = wa2.at[d * GLA_RANK:(d + 1) * GLA_RANK, d * GLA_DK:(d + 1) * GLA_DK].set(w_a2[d])
    return wa1.astype(BF16), wa2.astype(BF16), b_a.reshape(1, 2 * GLA_DK)


def _ml_gate_weights(w_gate, b_gate):
    wg = jnp.zeros((3 * ML_INNER, 256), F32)
    bg = jnp.zeros((1, 256), F32)
    for d in range(2):
        wg = wg.at[:, d * 128:d * 128 + 2 * ML_HEADS].set(w_gate[d])
        bg = bg.at[0, d * 128:d * 128 + 2 * ML_HEADS].set(b_gate[d])
    return wg.astype(BF16), bg


def kernel(x, c, ctx, c_ctx, ada_w, ada_b, norm1_g, norm2_g, mlp_w1, mlp_w2, gla_w_in, gla_w_a1, gla_w_a2, gla_b_a, gla_g_head, gla_w_o, mlstm_w_up, mlstm_conv_w, mlstm_conv_b, mlstm_w_qkv, mlstm_w_gate, mlstm_b_gate, mlstm_g_norm, mlstm_skip, mlstm_w_down, pool_w, pool_b, pool_scale, final_g):
    batch, seq, d = x.shape
    ctx_len = ctx.shape[1]
    depth = ada_w.shape[0]
    assert d == D_MODEL and ctx_len == ROW_TILE and seq % ROW_TILE == 0 and batch < MOD_ROWS
    tokens = ctx_len + seq
    n = batch * tokens
    tpb = tokens // ROW_TILE
    ctx_row = batch

    cc = jnp.zeros((MOD_ROWS, d), F32).at[:batch].set(c).at[ctx_row].set(c_ctx)
    mod_all = _mod_table(cc, ada_w, ada_b).reshape(depth, MOD_ROWS, 6, d)

    h = jnp.concatenate([ctx, x], axis=1).reshape(n, d)
    vec = lambda w: _full_spec((1, w))
    all_tiles = batch * tpb
    all_mod = _mod_spec(lambda t: jnp.where(t % tpb == 0, ctx_row, t // tpb))
    lat_per = seq // ROW_TILE
    lat_tiles = batch * lat_per
    lat_src = lambda u: u + u // lat_per + 1
    lat_mod = _mod_spec(lambda u: u // lat_per)

    for i in range(depth):
        kind, j = i % N_MIXERS, i // N_MIXERS
        mod = mod_all[i]
        g1 = norm1_g[i].reshape(1, d)
        last = i == depth - 1
        fin_tiles, fin_src, fin_mod = (lat_tiles, lat_src, lat_mod) if last else (all_tiles, _identity, all_mod)
        if kind == 0:
            w_in = gla_w_in[j].astype(BF16)
            wa1, wa2, ba = _gla_gate_weights(gla_w_a1[j], gla_w_a2[j], gla_b_a[j])
            q, k, v, r, la = _row_call(
                _gla_proj_kernel, all_tiles,
                ins=(h, mod, g1, w_in, wa1, wa2, ba),
                in_specs=[_row_spec(d), all_mod, vec(d), _full_spec(w_in.shape),
                          _full_spec(wa1.shape), _full_spec(wa2.shape), vec(2 * GLA_DK)],
                outs=[(GLA_DK, F32), (GLA_DK, F32), (GLA_DV, BF16), (GLA_DV, BF16), (2 * GLA_DK, F32)],
                name="gla_proj")
            o = _gla_scan(q, k, v, la, batch, tokens)
            w_o = gla_w_o[j].astype(BF16)
            (h,) = _row_call(
                _gla_fin_kernel, fin_tiles,
                ins=(o, r, h, mod, gla_g_head[j].reshape(1, GLA_DV), w_o),
                in_specs=[_pair_spec(GLA_DV, fin_src), _row_spec(GLA_DV, fin_src), _row_spec(d, fin_src),
                          fin_mod, vec(GLA_DV), _full_spec(w_o.shape)],
                outs=[(d, F32)], name="gla_fin")
        elif kind == 1:
            w_up = mlstm_w_up[j].astype(BF16)
            xm, z = _row_call(
                _ml_proj_kernel, all_tiles,
                ins=(h, mod, g1, w_up),
                in_specs=[_row_spec(d), all_mod, vec(d), _full_spec(w_up.shape)],
                outs=[(ML_INNER, F32), (ML_INNER, BF16)], name="mlstm_proj")
            wbd = jnp.stack([_block_diag_tiles(mlstm_w_qkv[j, p]) for p in range(3)]).astype(BF16)
            wg, bg = _ml_gate_weights(mlstm_w_gate[j], mlstm_b_gate[j])
            xc, q, k, v, gates = _ml_conv(xm, mlstm_conv_w[j], mlstm_conv_b[j].reshape(1, ML_INNER),
                                          wbd, wg, bg, batch, tokens)
            hs = _ml_scan(q, k, v, gates, batch, tokens)
            w_down = mlstm_w_down[j].astype(BF16)
            (h,) = _row_call(
                _ml_fin_kernel, fin_tiles,
                ins=(hs, xc, z, h, mod, mlstm_g_norm[j].reshape(1, ML_INNER),
                     mlstm_skip[j].reshape(1, ML_INNER), w_down),
                in_specs=[_pair_spec(ML_INNER, fin_src), _row_spec(ML_INNER, fin_src),
                          _row_spec(ML_INNER, fin_src), _row_spec(d, fin_src), fin_mod,
                          vec(ML_INNER), vec(ML_INNER), _full_spec(w_down.shape)],
                outs=[(d, F32)], name="mlstm_fin")
        else:
            (a,) = _row_call(
                _norm_mod_kernel, all_tiles, ins=(h, mod, g1),
                in_specs=[_row_spec(d), all_mod, vec(d)], outs=[(d, F32)], name="pool_norm")
            h = _pool_mix(a, h, mod, pool_w[j], pool_b[j], pool_scale[j], batch, tokens, ctx_row)
            if last:
                h = h.reshape(batch, tokens, d)[:, ctx_len:, :].reshape(batch * seq, d)

        w1, w2 = mlp_w1[i].astype(BF16), mlp_w2[i].astype(BF16)
        (h,) = _row_call(
            functools.partial(_mlp_kernel, final=last), fin_tiles,
            ins=(h, mod, norm2_g[i].reshape(1, d), w1, w2, final_g.reshape(1, d)),
            in_specs=[_row_spec(d), fin_mod, vec(d), _full_spec(w1.shape), _full_spec(w2.shape), vec(d)],
            outs=[(d, F32)], name="mlp")

    return h.reshape(batch, seq, d)
```

```python
import functools

import numpy as np
import jax
import jax.numpy as jnp
from jax import lax
from jax.experimental import pallas as pl
from jax.experimental.pallas import tpu as pltpu

F32 = jnp.float32
BF16 = jnp.bfloat16
HIGHEST = lax.Precision.HIGHEST

D_MODEL = 1024
D_FF = 4 * D_MODEL
N_MIXERS = 3
EPS = 1e-6
GRID_W = 64
ROW_TILE = 256
TILES_PER_STEP = 2
CHUNK = 64
MOD_ROWS = 16

GLA_HEADS = 4
GLA_DK = D_MODEL // 2
GLA_DV = D_MODEL
GLA_HK = GLA_DK // GLA_HEADS
GLA_HV = GLA_DV // GLA_HEADS
GLA_RANK = 16
GLA_TAU = 16.0

ML_HEADS = 4
ML_INNER = 2 * D_MODEL
ML_HD = ML_INNER // ML_HEADS
ML_BLOCK = 4
ML_CONV = 4
ML_BD = 256
HALO = 8

POOL_WINDOWS = (2, 4, 8, 16)
POOL_GROUP = D_MODEL // len(POOL_WINDOWS)
POOL_UNROLL = 8

VMEM_LIMIT = 56 * 1024 * 1024
VMEM_LIMIT_WIDE = 62 * 1024 * 1024

NT_DIMS = (((1,), (1,)), ((), ()))
TN_DIMS = (((0,), (0,)), ((), ()))


def _params(*sem, vmem_limit=VMEM_LIMIT, **kw):
    return pltpu.CompilerParams(dimension_semantics=sem, vmem_limit_bytes=vmem_limit, **kw)


def _silu(x):
    return x * jax.nn.sigmoid(x)


def _log_sigmoid(z):
    return jnp.minimum(z, 0.0) - jnp.log(1.0 + jnp.exp(-jnp.abs(z)))


def _rms(x):
    return x * lax.rsqrt(jnp.mean(x * x, axis=-1, keepdims=True) + EPS)


def _norm_mod(h, g, shift, scale):
    return (_rms(h) * g) * (1.0 + scale) + shift


def _dot(a, b):
    return jnp.dot(a, b, preferred_element_type=F32)


def _dot_f32(a, b):
    return jnp.dot(a, b, precision=HIGHEST, preferred_element_type=F32)


def _split3(x):
    x1 = x.astype(BF16)
    r1 = x - x1.astype(F32)
    x2 = r1.astype(BF16)
    x3 = (r1 - x2.astype(F32)).astype(BF16)
    return x1, x2, x3


def _dot_01(m, x):
    return sum(_dot(m, piece) for piece in _split3(x))


def _mod_kernel(cc_ref, w_ref, b_ref, o_ref):
    s = _silu(cc_ref[...])
    o_ref[0] = _dot_f32(s, w_ref[0]) + b_ref[0]


def _mod_table(cc, ada_w, ada_b):
    depth, d, n = ada_w.shape
    tn = 1536
    return pl.pallas_call(
        _mod_kernel,
        grid=(depth, n // tn),
        in_specs=[
            pl.BlockSpec((MOD_ROWS, d), lambda i, j: (0, 0)),
            pl.BlockSpec((1, d, tn), lambda i, j: (i, 0, j)),
            pl.BlockSpec((1, 1, tn), lambda i, j: (i, 0, j)),
        ],
        out_specs=pl.BlockSpec((1, MOD_ROWS, tn), lambda i, j: (i, 0, j)),
        out_shape=jax.ShapeDtypeStruct((depth, MOD_ROWS, n), F32),
        compiler_params=_params("arbitrary", "arbitrary"),
        name="mod_table",
    )(cc, ada_w, ada_b.reshape(depth, 1, n))


def _identity(t):
    return t


def _row_spec(width, src=_identity, tile=_identity):
    return pl.BlockSpec((ROW_TILE, width), lambda t: (src(tile(t)), 0))


def _full_spec(shape):
    nd = len(shape)
    return pl.BlockSpec(shape, lambda t: (0,) * nd)


def _mod_spec(mod_row, tile=_identity):
    return pl.BlockSpec((1, 6, D_MODEL), lambda t: (mod_row(tile(t)), 0, 0))


def _round_robin(*chains):
    chains = list(chains)
    while chains:
        for chain in list(chains):
            if next(chain, StopIteration) is StopIteration:
                chains.remove(chain)


def _take_h(tile, tiled, ctx_period):
    if ctx_period is None:
        return tiled[0][...], tiled[1:]
    is_ctx = tile % ctx_period == 0
    return jnp.where(is_ctx, tiled[0][...], tiled[1][...]), tiled[2:]


def _gla_proj_chain(tile, tiled, shared, outs, *, ctx_period):
    h, (mod_ref,) = _take_h(tile, tiled, ctx_period)
    g_ref, win_ref, wa1_ref, wa2_ref, ba_ref = shared
    q_ref, k_ref, v_ref, r_ref, la_ref = outs
    a = _norm_mod(h, g_ref[...], mod_ref[0, 0:1, :], mod_ref[0, 1:2, :]).astype(BF16)
    yield
    y = _dot(a, win_ref[...])
    q_ref[...] = y[:, :GLA_DK] * (GLA_HK ** -0.5)
    k_ref[...] = y[:, GLA_DK:2 * GLA_DK]
    v_ref[...] = y[:, 2 * GLA_DK:2 * GLA_DK + GLA_DV].astype(BF16)
    r_ref[...] = _silu(y[:, 2 * GLA_DK + GLA_DV:]).astype(r_ref.dtype)
    yield
    t = _dot(a, wa1_ref[...]).astype(BF16)
    z = _dot(t, wa2_ref[...]) + ba_ref[...]
    la_ref[...] = _log_sigmoid(z) * (1.0 / GLA_TAU)


def _gla_gated(of_ref, ob_ref, gate_ref, gh_ref):
    o = of_ref[...].astype(F32) + ob_ref[...].astype(F32)
    parts = [_rms(o[:, i * GLA_HV:(i + 1) * GLA_HV]) for i in range(GLA_HEADS)]
    on = jnp.concatenate(parts, axis=-1) * gh_ref[...]
    return (on * gate_ref[...].astype(F32)).astype(BF16)


def _ml_gated(hf_ref, hb_ref, xc_ref, gate_ref, gn_ref, skip_ref):
    hs = hf_ref[...].astype(F32) + hb_ref[...].astype(F32)
    parts = []
    for i in range(ML_HEADS):
        o = hs[:, i * ML_HD:(i + 1) * ML_HD]
        parts.append(_rms(o - jnp.mean(o, axis=-1, keepdims=True)))
    hn = jnp.concatenate(parts, axis=-1) * gn_ref[...]
    return ((hn + skip_ref[...] * xc_ref[...].astype(F32)) * gate_ref[...].astype(F32)).astype(BF16)


def _layer_tail_chain(tile, tiled, shared, outs, *, mixer, tail, ctx_period):
    h, tiled = _take_h(tile, tiled, ctx_period)
    mod_ref, modn_ref = tiled[:2]
    if mixer == "gla":
        y = _gla_gated(*tiled[2:], shared[0])
        w_ref, shared = shared[1], shared[2:]
    elif mixer == "mlstm":
        y = _ml_gated(*tiled[2:], shared[0], shared[1])
        w_ref, shared = shared[2], shared[3:]
    if mixer is not None:
        yield
        h = h + mod_ref[0, 2:3, :] * _dot(y, w_ref[...])
    g2_ref, w1_ref, w2_ref, aux_ref = shared
    a = _norm_mod(h, g2_ref[...], mod_ref[0, 3:4, :], mod_ref[0, 4:5, :]).astype(BF16)
    yield
    u = jnp.square(jnp.maximum(_dot(a, w1_ref[...]), 0.0)).astype(BF16)
    yield
    h = h + mod_ref[0, 5:6, :] * _dot(u, w2_ref[...])
    if tail == "final":
        h = _rms(h) * aux_ref[...]
    outs[0][...] = h
    if tail == "norm":
        outs[1][...] = _norm_mod(h, aux_ref[...], modn_ref[0, 0:1, :], modn_ref[0, 1:2, :])


def _ml_proj_chain(tile, tiled, shared, outs):
    h_ref, mod_ref = tiled
    g_ref, wup_ref = shared
    xm_ref, z_ref = outs
    a = _norm_mod(h_ref[...], g_ref[...], mod_ref[0, 0:1, :], mod_ref[0, 1:2, :]).astype(BF16)
    yield
    y = _dot(a, wup_ref[...])
    xm_ref[...] = y[:, :ML_INNER]
    z_ref[...] = _silu(y[:, ML_INNER:]).astype(z_ref.dtype)


def _rows_kernel(*refs, chain, per_step, n_tiled, n_out, n_scratch, **static):
    n_in = len(refs) - n_out - per_step * n_scratch
    shared = refs[per_step * n_tiled:n_in]
    out_refs = refs[n_in:n_in + n_out]
    scratch = refs[n_in + n_out:]
    chains = []
    for part in range(per_step):
        tiled = refs[part * n_tiled:(part + 1) * n_tiled]
        outs = tuple(o.at[pl.ds(part * ROW_TILE, ROW_TILE)] for o in out_refs)
        tile = pl.program_id(0) * per_step + part
        if n_scratch:
            static = dict(static, scratch=scratch[part * n_scratch:(part + 1) * n_scratch])
        chains.append(chain(tile, tiled, shared, outs, **static))
    _round_robin(*chains)


def _row_call(chain, n_tiles, tiled_ins, tiled_specs, shared_ins, shared_specs, outs, name,
              per_step=TILES_PER_STEP, scratch_shapes=(), vmem_limit=VMEM_LIMIT, **static):
    assert n_tiles % per_step == 0
    in_specs = [spec for part in range(per_step)
                for spec in tiled_specs(lambda t, part=part: t * per_step + part)]
    rows = per_step * ROW_TILE
    return pl.pallas_call(
        functools.partial(_rows_kernel, chain=chain, per_step=per_step, n_tiled=len(tiled_ins),
                          n_out=len(outs), n_scratch=len(scratch_shapes), **static),
        grid=(n_tiles // per_step,),
        in_specs=in_specs + list(shared_specs),
        out_specs=[pl.BlockSpec((rows, w), lambda t: (t, 0)) for w, _ in outs],
        out_shape=[jax.ShapeDtypeStruct((n_tiles * ROW_TILE, w), dt) for w, dt in outs],
        scratch_shapes=list(scratch_shapes) * per_step,
        compiler_params=_params("arbitrary", vmem_limit=vmem_limit),
        name=name,
    )(*(tuple(tiled_ins) * per_step + tuple(shared_ins)))


def _tri_pair(length, chunk, dtype):
    t = np.arange(length)
    same = (t[:, None] // chunk) == (t[None, :] // chunk)
    lower = (t[None, :] <= t[:, None]) & same
    upper = (t[None, :] >= t[:, None]) & same
    return jnp.asarray(np.stack([lower, upper]).astype(np.float32), dtype=dtype)


def _gla_block(fwd, q_ref, k_ref, v_ref, la_ref, tri, o_ref, st_ref):
    n_chunks = ROW_TILE // CHUNK
    assert n_chunks == 4
    rows = [slice(c * CHUNK, (c + 1) * CHUNK) for c in range(n_chunks)]
    g = la_ref[0]
    cum = _dot_01(tri, g)
    yield
    tot =[jnp.sum(g[r], axis=0, keepdims=True) for r in rows]
    dec = [jnp.exp(t) for t in tot]
    tot_rows = jnp.concatenate([jnp.broadcast_to(t, (CHUNK, t.shape[1])) for t in tot], axis=0)
    q = q_ref[0]
    k = k_ref[0]
    q_dec = q * jnp.exp(cum)
    k_inv = (k * jnp.exp(-cum)).astype(BF16)
    k_rem = k * jnp.exp(tot_rows - cum)

    d12 = dec[1] * dec[2]
    before = [None, dec[0], dec[0] * dec[1], dec[0] * d12]
    after = [d12 * dec[3], dec[2] * dec[3], dec[3], None]
    head, tail = (before, after) if fwd else (after, before)
    scaled = lambda x, f: x if f is None else x * f
    q_in = jnp.concatenate([scaled(q_dec[r], head[c]) for c, r in enumerate(rows)], axis=0).astype(BF16)
    k_out = jnp.concatenate([scaled(k_rem[r], tail[c]) for c, r in enumerate(rows)], axis=0).astype(BF16)
    dec_all = d12 * dec[0] * dec[3]

    between = {(0, 2): dec[1], (1, 3): dec[2], (0, 3): d12}
    def keys_for(c):
        parts = []
        for c2, r in enumerate(rows):
            if c2 == c:
                parts.append(k_inv[r])
            else:
                mid = between.get((min(c, c2), max(c, c2)))
                parts.append((k_rem[r] if mid is None else k_rem[r] * mid).astype(BF16))
        return jnp.concatenate(parts, axis=0)
    keys = [keys_for(c) for c in range(n_chunks)]
    q_dec = q_dec.astype(BF16)
    yield

    pos =lax.broadcasted_iota(jnp.int32, (ROW_TILE, ROW_TILE), 0)
    src = lax.broadcasted_iota(jnp.int32, (ROW_TILE, ROW_TILE), 1)
    keep = pos >= src if fwd else pos <= src

    for i in range(GLA_HEADS):
        kc = slice(i * GLA_HK, (i + 1) * GLA_HK)
        vc = slice(i * GLA_HV, (i + 1) * GLA_HV)
        v = v_ref[0, :, vc]
        att = jnp.concatenate(
            [lax.dot_general(q_dec[r, kc], keys[c][:, kc], NT_DIMS, preferred_element_type=F32)
             for c, r in enumerate(rows)], axis=0)
        att = jnp.where(keep, att, 0.0).astype(BF16)
        st = st_ref[i]
        o_ref[0, :, vc] = (_dot(att, v) + lax.dot_general(
            q_in[:, kc], st.astype(BF16), NT_DIMS, preferred_element_type=F32)).astype(o_ref.dtype)
        st_ref[i] = st * dec_all[:, kc] + lax.dot_general(
            v, k_out[:, kc], TN_DIMS, preferred_element_type=F32)
        yield


def _gla_scan_kernel(qf_ref, kf_ref, vf_ref, laf_ref, qb_ref, kb_ref, vb_ref, lab_ref, tri_ref,
                     of_ref, ob_ref, st_ref):
    @pl.when(pl.program_id(1) == 0)
    def _():
        st_ref[...] = jnp.zeros_like(st_ref)

    _round_robin(
        _gla_block(True, qf_ref, kf_ref, vf_ref, laf_ref, tri_ref[0], of_ref, st_ref.at[0]),
        _gla_block(False, qb_ref, kb_ref, vb_ref, lab_ref, tri_ref[1], ob_ref, st_ref.at[1]))


def _scan_specs(n_blocks, width, lane_block=None):
    back = lambda s: jnp.where(s == 0, 0, n_blocks - s)
    lane = (0, 0) if lane_block is None else lane_block
    return (pl.BlockSpec((1, ROW_TILE, width), lambda b, s: (b, s, lane[0])),
            pl.BlockSpec((1, ROW_TILE, width), lambda b, s: (b, back(s), lane[1])))


def _gla_scan(q, k, v, la, batch, tokens):
    n_blocks = tokens // ROW_TILE
    q3, k3 = q.reshape(batch, tokens, GLA_DK), k.reshape(batch, tokens, GLA_DK)
    v3 = v.reshape(batch, tokens, GLA_DV)
    la3 = la.reshape(batch, tokens, 2 * GLA_DK)
    qk_f, qk_b = _scan_specs(n_blocks, GLA_DK)
    v_f, v_b = _scan_specs(n_blocks, GLA_DV)
    la_f, la_b = _scan_specs(n_blocks, GLA_DK, lane_block=(0, 1))
    o_f, o_b = pl.pallas_call(
        _gla_scan_kernel,
        grid=(batch, n_blocks),
        in_specs=[qk_f, qk_f, v_f, la_f, qk_b, qk_b, v_b, la_b,
                  pl.BlockSpec((2, ROW_TILE, ROW_TILE), lambda b, s: (0, 0, 0))],
        out_specs=[v_f, v_b],
        out_shape=[jax.ShapeDtypeStruct((batch, tokens, GLA_DV), BF16)] * 2,
        scratch_shapes=[pltpu.VMEM((2, GLA_HEADS, GLA_HV, GLA_HK), F32)],
        compiler_params=_params("arbitrary", "arbitrary"),
        name="gla_scan",
    )(q3, k3, v3, la3, q3, k3, v3, la3, _tri_pair(ROW_TILE, CHUNK, BF16))
    return o_f.reshape(batch * tokens, GLA_DV), o_b.reshape(batch * tokens, GLA_DV)


def _ml_block(q_ref, k_ref, v_ref, g_ref, tri, o_ref, c_ref, n_ref, m_ref):
    H = ML_HEADS
    keep = tri > 0.0
    g = g_ref[0]
    cum = _dot_01(tri, g)
    yield
    tot = jnp.sum(g, axis=0, keepdims=True)
    g_t = g.T
    cum_t = cum.T
    for i in range(H):
        hc = slice(i * ML_HD, (i + 1) * ML_HD)
        i_col, b_col = g[:, i:i + 1], cum[:, H + i:H + i + 1]
        i_row, b_row = g_t[i:i + 1, :], cum_t[H + i:H + i + 1, :]
        b_last = tot[:, H + i:H + i + 1]
        q = q_ref[0, :, hc]
        k = k_ref[0, :, hc]
        v = v_ref[0, :, hc]
        m_prev = m_ref[i, 0:1, 0:1]
        c_bar = c_ref[i]
        n_bar = n_ref[i]
        qk = lax.dot_general(q, k, NT_DIMS, preferred_element_type=F32)
        d_mat = jnp.where(keep, b_col - b_row + i_row, -jnp.inf)
        inter = b_col + m_prev
        m_row = jnp.maximum(jnp.max(d_mat, axis=-1, keepdims=True), inter)
        w_qk = qk * jnp.exp(d_mat - m_row)
        w_inter = jnp.exp(inter - m_row)
        num = _dot(w_qk.astype(BF16), v) + w_inter * _dot(q, c_bar.astype(BF16))
        den = (jnp.sum(w_qk, axis=-1, keepdims=True)
               + w_inter * jnp.sum(q.astype(F32) * n_bar, axis=-1, keepdims=True))
        o_ref[0, :, hc] = (num / jnp.maximum(jnp.abs(den), jnp.exp(-m_row))).astype(o_ref.dtype)
        yield
        carry_log = b_last + m_prev
        m_new = jnp.maximum(carry_log, jnp.max(b_last - b_row + i_row, axis=-1, keepdims=True))
        w_tok = jnp.exp(b_last - b_col + i_col - m_new)
        w_carry = jnp.exp(carry_log - m_new)
        k_w = k.astype(F32) * w_tok
        c_ref[i] = w_carry * c_bar + lax.dot_general(k_w.astype(BF16), v, TN_DIMS,
                                                     preferred_element_type=F32)
        n_ref[i] = w_carry * n_bar + jnp.sum(k_w, axis=0, keepdims=True)
        m_ref[i] = jnp.broadcast_to(m_new, m_ref.shape[1:])
        yield


def _ml_scan_kernel(qf_ref, kf_ref, vf_ref, gf_ref, qb_ref, kb_ref, vb_ref, gb_ref, tri_ref,
                    of_ref, ob_ref, c_ref, n_ref, m_ref):
    @pl.when(pl.program_id(1) == 0)
    def _():
        c_ref[...] = jnp.zeros_like(c_ref)
        n_ref[...] = jnp.zeros_like(n_ref)
        m_ref[...] = jnp.zeros_like(m_ref)

    _round_robin(
        _ml_block(qf_ref, kf_ref, vf_ref, gf_ref, tri_ref[0], of_ref, c_ref.at[0], n_ref.at[0], m_ref.at[0]),
        _ml_block(qb_ref, kb_ref, vb_ref, gb_ref, tri_ref[1], ob_ref, c_ref.at[1], n_ref.at[1], m_ref.at[1]))


def _ml_scan(q, k, v, gates, batch, tokens):
    n_blocks = tokens // ROW_TILE
    shape3 = (batch, tokens, ML_INNER)
    x_f, x_b = _scan_specs(n_blocks, ML_INNER)
    g_f, g_b = _scan_specs(n_blocks, 128, lane_block=(0, 1))
    q3, k3, v3, g3 = q.reshape(shape3), k.reshape(shape3), v.reshape(shape3), gates.reshape(batch, tokens, 256)
    o_f, o_b = pl.pallas_call(
        _ml_scan_kernel,
        grid=(batch, n_blocks),
        in_specs=[x_f, x_f, x_f, g_f, x_b, x_b, x_b, g_b,
                  pl.BlockSpec((2, ROW_TILE, ROW_TILE), lambda b, s: (0, 0, 0))],
        out_specs=[x_f, x_b],
        out_shape=[jax.ShapeDtypeStruct(shape3, BF16)] * 2,
        scratch_shapes=[pltpu.VMEM((2, ML_HEADS, ML_HD, ML_HD), F32),
                        pltpu.VMEM((2, ML_HEADS, 1, ML_HD), F32),
                        pltpu.VMEM((2, ML_HEADS, 8, 128), F32)],
        compiler_params=_params("arbitrary", "arbitrary"),
        name="mlstm_scan",
    )(q3, k3, v3, g3, q3, k3, v3, g3, _tri_pair(ROW_TILE, ROW_TILE, BF16))
    return o_f.reshape(batch * tokens, ML_INNER), o_b.reshape(batch * tokens, ML_INNER)


def _ml_conv_chain(tile, tiled, shared, outs, *, scratch, n_blocks):
    x_ref, xp_ref, xn_ref = tiled
    cw_ref, cb_ref, wbd_ref, wg_ref, bg_ref = shared
    xc_ref, q_ref, k_ref, v_ref, g_ref = outs
    (ext_ref,) = scratch
    t = tile % n_blocks
    prev_ok = t >= 2
    next_ok = jnp.logical_and(t >= 1, t < n_blocks - 1)
    ext_ref[0:HALO, :] = jnp.where(prev_ok, xp_ref[...], 0.0)
    ext_ref[HALO:HALO + ROW_TILE, :] = x_ref[...]
    ext_ref[HALO + ROW_TILE:, :] = jnp.where(next_ok, xn_ref[...], 0.0)
    gates = jnp.zeros((ROW_TILE, 256), F32)
    for cb in range(ML_INNER // ML_BD):
        cols = slice(cb * ML_BD, (cb + 1) * ML_BD)
        y = cb_ref[:, cols]
        for tap in range(ML_CONV):
            off = HALO + tap - ML_CONV // 2
            y = y + cw_ref[tap:tap + 1, cols] * ext_ref[off:off + ROW_TILE, cols]
        xc = _silu(y)
        xc_b = xc.astype(BF16)
        xc_ref[:, cols] = xc_b
        xm_b = ext_ref[HALO:HALO + ROW_TILE, cols].astype(BF16)
        yield
        q = _dot(xc_b, wbd_ref[0, cb])
        k = _dot(xc_b, wbd_ref[1, cb])
        v = _dot(xm_b, wbd_ref[2, cb])
        q_b, k_b, v_b = q.astype(BF16), k.astype(BF16), v.astype(BF16)
        q_ref[:, cols] = q_b
        k_ref[:, cols] = (k * (ML_HD ** -0.5)).astype(BF16)
        v_ref[:, cols] = v_b
        for j, part in enumerate((q_b, k_b, v_b)):
            gates = gates + _dot(part, wg_ref[j * ML_INNER + cb * ML_BD:j * ML_INNER + (cb + 1) * ML_BD, :])
        yield
    gates = gates + bg_ref[...]
    lane = lax.broadcasted_iota(jnp.int32, gates.shape, 1) % 128
    is_forget = jnp.logical_and(lane >= ML_HEADS, lane < 2 * ML_HEADS)
    g_ref[...] = jnp.where(is_forget, _log_sigmoid(gates), gates)


def _ml_conv(xm, conv_w, conv_b, wbd, wg, bg, batch, tokens):
    n_blocks = tokens // ROW_TILE
    per = ROW_TILE // HALO
    last = batch * tokens // HALO - 1
    halo = lambda index: pl.BlockSpec((HALO, ML_INNER), lambda t: (index(t), 0))
    full = lambda arr: _full_spec(arr.shape)
    return _row_call(
        _ml_conv_chain, batch * n_blocks,
        tiled_ins=(xm, xm, xm),
        tiled_specs=lambda tile: [_row_spec(ML_INNER, _identity, tile),
                                  halo(lambda t: jnp.maximum(tile(t) * per - 1, 0)),
                                  halo(lambda t: jnp.minimum((tile(t) + 1) * per, last))],
        shared_ins=(conv_w, conv_b, wbd, wg, bg),
        shared_specs=[full(conv_w), full(conv_b), full(wbd), full(wg), full(bg)],
        outs=[(ML_INNER, BF16)] * 4 + [(256, F32)], name="mlstm_conv",
        scratch_shapes=[pltpu.VMEM((ROW_TILE + 2 * HALO, ML_INNER), F32)], n_blocks=n_blocks)


def _pool_kernel(a_ref, h_ref, modl_ref, modc_ref, band_ref, wp_ref, bp_ref, sc_ref,
                 o_ref, y_ref, *, n_rows):
    group = pl.program_id(1)
    ctx = ROW_TILE
    wp = wp_ref[0]
    bp = bp_ref[0]
    sc = sc_ref[...]
    gate_l = modl_ref[0, 2:3, :]
    gate_c = modc_ref[0, 2:3, :]

    def mix(delta, res, gate):
        return res + gate * ((_dot(delta.astype(BF16), wp) + bp) * sc)

    def body(win):
        lo, hi = win // 2, win - 1 - win // 2

        def count(pos, length):
            return (jnp.minimum(pos + hi, length - 1) - jnp.maximum(pos - lo, 0) + 1).astype(F32)

        x = a_ref[0, 0:ctx, :]
        tpos = lax.broadcasted_iota(jnp.int32, (ctx, POOL_GROUP), 0)
        pooled = _dot_01(band_ref[0, 0], x) / count(tpos, ctx)
        o_ref[0, 0:ctx, :] = mix(pooled - x, h_ref[0, 0:ctx, :], gate_c)

        band = band_ref[0, 1]
        rows_per_tile = ROW_TILE // GRID_W
        n_tiles = n_rows // rows_per_tile
        y_ref[0:GRID_W, :] = jnp.zeros((GRID_W, POOL_GROUP), F32)

        def width_sum(tb, run):
            src = pl.ds(pl.multiple_of(ctx + tb * ROW_TILE, ROW_TILE), ROW_TILE)
            w = _dot_01(band, a_ref[0, src, :])
            for rr in range(rows_per_tile):
                run = run + w[rr * GRID_W:(rr + 1) * GRID_W]
                dst = pl.multiple_of((tb * rows_per_tile + rr + 1) * GRID_W, GRID_W)
                y_ref[pl.ds(dst, GRID_W), :] = run
            return run

        lax.fori_loop(0, n_tiles, width_sum, jnp.zeros((GRID_W, POOL_GROUP), F32), unroll=POOL_UNROLL)

        cpos = lax.broadcasted_iota(jnp.int32, (GRID_W, POOL_GROUP), 0)
        cnt_w = count(cpos, GRID_W)

        def height_mix(tb, carry):
            parts = []
            for rr in range(rows_per_tile):
                r = tb * rows_per_tile + rr
                top = pl.multiple_of((jnp.minimum(r + hi, n_rows - 1) + 1) * GRID_W, GRID_W)
                bot = pl.multiple_of(jnp.maximum(r - lo, 0) * GRID_W, GRID_W)
                total = y_ref[pl.ds(top, GRID_W), :] - y_ref[pl.ds(bot, GRID_W), :]
                parts.append(total / (cnt_w * count(r, n_rows)))
            rows = pl.ds(pl.multiple_of(ctx + tb * ROW_TILE, ROW_TILE), ROW_TILE)
            pooled = jnp.concatenate(parts, axis=0)
            o_ref[0, rows, :] = mix(pooled - a_ref[0, rows, :], h_ref[0, rows, :], gate_l)
            return carry

        lax.fori_loop(0, n_tiles, height_mix, 0, unroll=POOL_UNROLL)

    for gi, win in enumerate(POOL_WINDOWS):
        pl.when(group == gi)(functools.partial(body, win))


def _pool_bands():
    bands = np.zeros((len(POOL_WINDOWS), 2, ROW_TILE, ROW_TILE), np.float32)
    t = np.arange(ROW_TILE)
    for gi, win in enumerate(POOL_WINDOWS):
        lo, hi = win // 2, win - 1 - win // 2
        near = (t[None, :] - t[:, None] >= -lo) & (t[None, :] - t[:, None] <= hi)
        same_row = (t[None, :] // GRID_W) == (t[:, None] // GRID_W)
        bands[gi, 0] = near
        bands[gi, 1] = near & same_row
    return jnp.asarray(bands, dtype=BF16)


def _pool_mix(a, h, mod, w_pool, b_pool, scale, batch, tokens, ctx_row):
    n_groups = len(POOL_WINDOWS)
    lat = tokens - ROW_TILE
    a3, h3 = a.reshape(batch, tokens, D_MODEL), h.reshape(batch, tokens, D_MODEL)
    slab = pl.BlockSpec((1, tokens, POOL_GROUP), lambda b, g: (b, 0, g))
    out = pl.pallas_call(
        functools.partial(_pool_kernel, n_rows=lat // GRID_W),
        grid=(batch, n_groups),
        in_specs=[
            slab, slab,
            pl.BlockSpec((1, 6, POOL_GROUP), lambda b, g: (b, 0, g)),
            pl.BlockSpec((1, 6, POOL_GROUP), lambda b, g: (ctx_row, 0, g)),
            pl.BlockSpec((1, 2, ROW_TILE, ROW_TILE), lambda b, g: (g, 0, 0, 0)),
            pl.BlockSpec((1, POOL_GROUP, POOL_GROUP), lambda b, g: (g, 0, 0)),
            pl.BlockSpec((1, 1, POOL_GROUP), lambda b, g: (g, 0, 0)),
            pl.BlockSpec((1, POOL_GROUP), lambda b, g: (0, g)),
        ],
        out_specs=slab,
        out_shape=jax.ShapeDtypeStruct((batch, tokens, D_MODEL), F32),
        scratch_shapes=[pltpu.VMEM((lat + GRID_W, POOL_GROUP), F32)],
        compiler_params=_params("arbitrary", "arbitrary"),
        name="pool_mix",
    )(a3, h3, mod, mod, _pool_bands(), w_pool.astype(BF16),
      b_pool.reshape(n_groups, 1, POOL_GROUP), scale.reshape(1, D_MODEL))
    return out.reshape(batch * tokens, D_MODEL)


def _block_diag_tiles(w):
    per = ML_BD // ML_BLOCK
    w = w.reshape(-1, per, ML_BLOCK, ML_BLOCK)
    eye = jnp.eye(per, dtype=w.dtype)
    t = w[:, :, :, None, :] * eye[None, :, None, :, None]
    return t.reshape(-1, ML_BD, ML_BD)


def _gla_gate_weights(w_a1, w_a2, b_a):
    wa1 = jnp.zeros((D_MODEL, 128), F32)
    wa2 = jnp.zeros((128, 2 * GLA_DK), F32)
    for d in range(2):
        wa1 = wa1.at[:, d * GLA_RANK:(d + 1) * GLA_RANK].set(w_a1[d])
        wa2 = wa2.at[d * GLA_RANK:(d + 1) * GLA_RANK, d * GLA_DK:(d + 1) * GLA_DK].set(w_a2[d])
    return wa1.astype(BF16), wa2.astype(BF16), b_a.reshape(1, 2 * GLA_DK)


def _ml_gate_weights(w_gate, b_gate):
    pad = ((0, 0), (0, 0), (0, 128 - 2 * ML_HEADS))
    wg = jnp.pad(w_gate.astype(BF16), pad)
    bg = jnp.pad(b_gate, pad[1:])
    return jnp.concatenate([wg[0], wg[1]], axis=1), bg.reshape(1, 256)


def kernel(x, c, ctx, c_ctx, ada_w, ada_b, norm1_g, norm2_g, mlp_w1, mlp_w2, gla_w_in, gla_w_a1, gla_w_a2, gla_b_a, gla_g_head, gla_w_o, mlstm_w_up, mlstm_conv_w, mlstm_conv_b, mlstm_w_qkv, mlstm_w_gate, mlstm_b_gate, mlstm_g_norm, mlstm_skip, mlstm_w_down, pool_w, pool_b, pool_scale, final_g):
    batch, seq, d = x.shape
    ctx_len = ctx.shape[1]
    depth = ada_w.shape[0]
    assert d == D_MODEL and ctx_len == ROW_TILE and seq % ROW_TILE == 0 and batch < MOD_ROWS
    tokens = ctx_len + seq
    n = batch * tokens
    tpb = tokens // ROW_TILE
    ctx_row = batch

    cc = jnp.zeros((MOD_ROWS, d), F32).at[:batch].set(c).at[ctx_row].set(c_ctx)
    mod_all = _mod_table(cc, ada_w, ada_b).reshape(depth, MOD_ROWS, 6, d)

    vec = lambda w: _full_spec((1, w))
    all_tiles = batch * tpb
    all_mod = lambda t: jnp.where(t % tpb == 0, ctx_row, t // tpb)
    lat_per = seq // ROW_TILE
    lat_tiles = batch * lat_per
    lat_src = lambda u: u + u // lat_per + 1
    lat_mod = lambda u: u // lat_per

    w1_all, w2_all = mlp_w1.astype(BF16), mlp_w2.astype(BF16)
    split_ins = (ctx.reshape(batch * ctx_len, d), x.reshape(batch * seq, d))
    split_src = (lambda t: t // tpb, lambda t: (t // tpb) * lat_per + jnp.maximum(t % tpb - 1, 0))
    h = None
    a = None
    for i in range(depth):
        kind, j = i % N_MIXERS, i // N_MIXERS
        mod = mod_all[i]
        g1 = norm1_g[i].reshape(1, d)
        last = i == depth - 1
        fin_tiles, fin_src, fin_mod = (lat_tiles, lat_src, lat_mod) if last else (all_tiles, _identity, all_mod)
        split = h is None and kind == 0 and not last
        if h is None and not split:
            h = jnp.concatenate([ctx, x], axis=1).reshape(n, d)
        ctx_period = tpb if split else None
        h_ins, h_src = (split_ins, split_src) if split else ((h,), (_identity,))
        fin_h_src = split_src if split else (fin_src,)
        if kind == 0:
            w_in = gla_w_in[j].astype(BF16)
            wa1, wa2, ba = _gla_gate_weights(gla_w_a1[j], gla_w_a2[j], gla_b_a[j])
            q, k, v, r, la = _row_call(
                _gla_proj_chain, all_tiles,
                tiled_ins=h_ins + (mod,),
                tiled_specs=lambda tile: [_row_spec(d, src, tile) for src in h_src] + [_mod_spec(all_mod, tile)],
                shared_ins=(g1, w_in, wa1, wa2, ba),
                shared_specs=[vec(d), _full_spec(w_in.shape), _full_spec(wa1.shape), _full_spec(wa2.shape),
                              vec(2 * GLA_DK)],
                outs=[(GLA_DK, F32), (GLA_DK, F32), (GLA_DV, BF16), (GLA_DV, BF16), (2 * GLA_DK, F32)],
                name="gla_proj", ctx_period=ctx_period)
            o_f, o_b = _gla_scan(q, k, v, la, batch, tokens)
            w_o = gla_w_o[j].astype(BF16)
            mixer = "gla"
            mixer_rows = [(o_f, GLA_DV), (o_b, GLA_DV), (r, GLA_DV)]
            mixer_shared = (gla_g_head[j].reshape(1, GLA_DV), w_o)
            mixer_shared_specs = [vec(GLA_DV), _full_spec(w_o.shape)]
        elif kind == 1:
            w_up = mlstm_w_up[j].astype(BF16)
            xm, z = _row_call(
                _ml_proj_chain, all_tiles,
                tiled_ins=(h, mod),
                tiled_specs=lambda tile: [_row_spec(d, _identity, tile), _mod_spec(all_mod, tile)],
                shared_ins=(g1, w_up), shared_specs=[vec(d), _full_spec(w_up.shape)],
                outs=[(ML_INNER, F32), (ML_INNER, BF16)], name="mlstm_proj")
            wbd = jnp.stack([_block_diag_tiles(mlstm_w_qkv[j, p].astype(BF16)) for p in range(3)])
            wg, bg = _ml_gate_weights(mlstm_w_gate[j], mlstm_b_gate[j])
            xc, q, k, v, gates = _ml_conv(xm, mlstm_conv_w[j], mlstm_conv_b[j].reshape(1, ML_INNER),
                                          wbd, wg, bg, batch, tokens)
            h_f, h_b = _ml_scan(q, k, v, gates, batch, tokens)
            w_down = mlstm_w_down[j].astype(BF16)
            mixer = "mlstm"
            mixer_rows = [(h_f, ML_INNER), (h_b, ML_INNER), (xc, ML_INNER), (z, ML_INNER)]
            mixer_shared = (mlstm_g_norm[j].reshape(1, ML_INNER), mlstm_skip[j].reshape(1, ML_INNER), w_down)
            mixer_shared_specs = [vec(ML_INNER), vec(ML_INNER), _full_spec(w_down.shape)]
        else:
            assert a is not None, "a pooling layer follows another layer"
            h = _pool_mix(a, h, mod, pool_w[j], pool_b[j], pool_scale[j], batch, tokens, ctx_row)
            if last:
                h = h.reshape(batch, tokens, d)[:, ctx_len:, :].reshape(batch * seq, d)
            mixer = None
            mixer_rows, mixer_shared, mixer_shared_specs = [], (), []
            h_ins, fin_h_src = (h,), (_identity,)

        next_pool = not last and (i + 1) % N_MIXERS == 2
        tail = "final" if last else "norm" if next_pool else None
        aux = norm1_g[i + 1] if next_pool else final_g
        mod_next = mod_all[i + 1] if next_pool else mod
        layer_w1 = pl.BlockSpec((None, d, D_FF), lambda t, i=i: (i, 0, 0))
        layer_w2 = pl.BlockSpec((None, D_FF, d), lambda t, i=i: (i, 0, 0))
        res = _row_call(
            _layer_tail_chain, fin_tiles,
            tiled_ins=h_ins + (mod, mod_next) + tuple(arr for arr, _ in mixer_rows),
            tiled_specs=lambda tile: ([_row_spec(d, src, tile) for src in fin_h_src]
                                      + [_mod_spec(fin_mod, tile)] * 2
                                      + [_row_spec(w, fin_src, tile) for _, w in mixer_rows]),
            shared_ins=mixer_shared + (norm2_g[i].reshape(1, d), w1_all, w2_all, aux.reshape(1, d)),
            shared_specs=mixer_shared_specs + [vec(d), layer_w1, layer_w2, vec(d)],
            outs=[(d, F32)] * (2 if next_pool else 1), name="layer_tail",
            vmem_limit=VMEM_LIMIT_WIDE if mixer == "mlstm" else VMEM_LIMIT,
            mixer=mixer, tail=tail, ctx_period=ctx_period)
        h = res[0]
        a = res[1] if next_pool else None

    return h.reshape(batch, seq, d)
```
